```python
import jax, jax.numpy as jnp
from jax import lax
import numpy as np

D_MODEL = 2048
BATCH = 2
SEQ = 8192
DEPTH = 1
DEC_BATCH = 16
DEC_SEQ = 16
PAST_LEN = 4096

CHUNK = 64
Q_BLOCK = 128
EPS = 1e-6
MLA_HEADS = 8
Q_LORA = 512
KV_LORA = 512
QK_NOPE = 128
QK_ROPE = 64
V_HEAD = 128
ROPE_THETA = 10000.0
GDN_HEADS = 8
GDN_DK = 128
GDN_DV = 128
GDN_CONV = 4
GDN_CHUNK = 64
GDN_CONV_DIM = 2 * GDN_HEADS * GDN_DK + GDN_HEADS * GDN_DV
D_FF = 5632
FFN_CONV = 3
MLA_OUT = MLA_HEADS * V_HEAD
GDN_OUT = GDN_HEADS * GDN_DV
D_MIX = MLA_OUT + GDN_OUT
D_IN = Q_LORA + KV_LORA + QK_ROPE + GDN_CONV_DIM + GDN_OUT + 2 * GDN_HEADS

kernel_name = 'hybrid_mla_gdn_convffn_stream_step'


def rmsnorm(x, g):
    xf = x.astype(jnp.float32)
    y = xf * lax.rsqrt(jnp.mean(xf * xf, axis=-1, keepdims=True) + EPS)
    return (y * g.astype(jnp.float32)).astype(x.dtype)


def l2norm(x):
    xf = x.astype(jnp.float32)
    return xf * lax.rsqrt(jnp.sum(xf * xf, axis=-1, keepdims=True) + EPS)


def rope_cos_sin(pos):
    half = QK_ROPE // 2
    inv = 1.0 / (ROPE_THETA ** (jnp.arange(half, dtype=jnp.float32) / half))
    ang = pos.astype(jnp.float32)[:, None] * inv[None, :]
    return jnp.cos(ang), jnp.sin(ang)


def apply_rope(x, cos, sin):
    half = QK_ROPE // 2
    xf = x.astype(jnp.float32)
    x1, x2 = xf[..., :half], xf[..., half:]
    return jnp.concatenate([x1 * cos - x2 * sin, x2 * cos + x1 * sin], axis=-1).astype(x.dtype)


def causal_dwconv(x_hist, w):
    C = x_hist.shape[-1]
    return lax.conv_general_dilated(x_hist, w[:, None, :].astype(x_hist.dtype), window_strides=(1,),
                                    padding='VALID', dimension_numbers=('NWC', 'WIO', 'NWC'),
                                    feature_group_count=C)


def split_in(proj):
    offs = np.cumsum([Q_LORA, KV_LORA, QK_ROPE, GDN_CONV_DIM, GDN_OUT, GDN_HEADS]).tolist()
    return jnp.split(proj, offs, axis=-1)


def mla_attention(q_nope, q_rope, k_nope, k_rope, c_kv, w_uv, past):
    B, L, H, _ = q_nope.shape
    T = k_nope.shape[1]
    qb = min(Q_BLOCK, L)
    nb = L // qb
    scale = (QK_NOPE + QK_ROPE) ** -0.5
    k_chunk = jnp.arange(T) // CHUNK
    q_pos = past + jnp.arange(L)

    def block(args):
        qn, qr, qp = args
        s = (jnp.einsum('bqhn,bkhn->bhqk', qn, k_nope, preferred_element_type=jnp.float32)
             + jnp.einsum('bqhr,bkr->bhqk', qr, k_rope, preferred_element_type=jnp.float32)) * scale
        visible = k_chunk[None, :] <= (qp // CHUNK)[:, None]
        p = jax.nn.softmax(jnp.where(visible, s, -jnp.inf), axis=-1).astype(c_kv.dtype)
        o_lat = jnp.einsum('bhqk,bkc->bqhc', p, c_kv)
        return jnp.einsum('bqhc,chd->bqhd', o_lat, w_uv)

    qn_b = q_nope.reshape(B, nb, qb, H, QK_NOPE).transpose(1, 0, 2, 3, 4)
    qr_b = q_rope.reshape(B, nb, qb, H, QK_ROPE).transpose(1, 0, 2, 3, 4)
    out = lax.map(block, (qn_b, qr_b, q_pos.reshape(nb, qb)))
    return out.transpose(1, 0, 2, 3, 4).reshape(B, L, H * V_HEAD)


def gated_delta_rule(q, k, v, g, beta, S0):
    B, L, H, DK = k.shape
    DV = v.shape[-1]
    C = min(GDN_CHUNK, L)
    n = L // C
    ch4 = lambda t: t.reshape(B, n, C, H, t.shape[-1]).transpose(1, 0, 3, 2, 4)
    ch3 = lambda t: t.reshape(B, n, C, H).transpose(1, 0, 3, 2)
    q, k, v = ch4(q), ch4(k), ch4(v)
    gc = jnp.cumsum(ch3(g), axis=-1)
    beta = ch3(beta)
    idx = jnp.arange(C)
    lower_incl = idx[:, None] >= idx[None, :]
    strict = idx[:, None] > idx[None, :]
    decay = jnp.exp(jnp.where(lower_incl, gc[..., :, None] - gc[..., None, :], -jnp.inf))
    kk = jnp.einsum('nbhik,nbhjk->nbhij', k, k)
    A = jnp.where(strict, beta[..., :, None] * kk * decay, 0.0)
    rhs = jnp.concatenate([v * beta[..., None], k * (beta * jnp.exp(gc))[..., None]], axis=-1)
    sol = lax.linalg.triangular_solve(A + jnp.eye(C, dtype=A.dtype), rhs, left_side=True, lower=True,
                                      unit_diagonal=True)
    u, w = sol[..., :DV], sol[..., DV:]

    def step(S, inp):
        qc, kc, uc, wc, gcc, dc = inp
        v_new = uc - jnp.einsum('bhck,bhkv->bhcv', wc, S)
        intra = jnp.einsum('bhik,bhjk->bhij', qc, kc) * dc
        o = (jnp.einsum('bhck,bhkv->bhcv', qc * jnp.exp(gcc)[..., None], S)
             + jnp.einsum('bhij,bhjv->bhiv', intra, v_new))
        g_last = gcc[..., -1]
        S = (S * jnp.exp(g_last)[..., None, None]
             + jnp.einsum('bhck,bhcv->bhkv', kc * jnp.exp(g_last[..., None] - gcc)[..., None], v_new))
        return S, o

    S_fin, o = lax.scan(step, S0, (q, k, u, w, gc, decay))
    return o.transpose(1, 0, 3, 2, 4).reshape(B, L, H, DV), S_fin


def hybrid_layer(x, lat_past, krope_past, conv_past, s_past, ffn_past, lw):
    B, L, _ = x.shape
    P = lat_past.shape[1]
    f32 = jnp.float32
    h = rmsnorm(x, lw['g_attn_norm'])
    q_a, kv_a, k_r, qkv, z, a_logit, b_logit = split_in(h @ lw['w_in'])

    q = (rmsnorm(q_a, lw['g_q_lat']) @ lw['w_q_up']).reshape(B, L, MLA_HEADS, QK_NOPE + QK_ROPE)
    cos, sin = rope_cos_sin(P + jnp.arange(L))
    q_nope = rmsnorm(q[..., :QK_NOPE], lw['g_q_nope'])
    q_rope = apply_rope(rmsnorm(q[..., QK_NOPE:], lw['g_q_rope']), cos[:, None, :], sin[:, None, :])
    c_kv = rmsnorm(kv_a, lw['g_kv_lat'])
    k_rope = apply_rope(rmsnorm(k_r, lw['g_k_rope']), cos, sin)
    lat_all = jnp.concatenate([lat_past, c_kv], axis=1)
    kr_all = jnp.concatenate([krope_past, k_rope], axis=1)
    w_uk = lw['w_kv_up'][..., :QK_NOPE]
    w_uv = lw['w_kv_up'][..., QK_NOPE:]
    k_nope = rmsnorm(jnp.einsum('btc,chn->bthn', lat_all, w_uk), lw['g_k_nope'])
    o_a = mla_attention(q_nope, q_rope, k_nope, kr_all, lat_all, w_uv, P)

    qkv_hist = jnp.concatenate([conv_past, qkv], axis=1)
    qkv_c = jax.nn.silu(causal_dwconv(qkv_hist, lw['w_gdn_conv']))
    nk = GDN_HEADS * GDN_DK
    gq = l2norm(qkv_c[..., :nk].reshape(B, L, GDN_HEADS, GDN_DK)) * (GDN_DK ** -0.5)
    gk = l2norm(qkv_c[..., nk:2 * nk].reshape(B, L, GDN_HEADS, GDN_DK))
    gv = qkv_c[..., 2 * nk:].reshape(B, L, GDN_HEADS, GDN_DV).astype(f32)
    beta = jax.nn.sigmoid(b_logit.astype(f32))
    g = -jnp.exp(lw['a_log'].astype(f32)) * jax.nn.softplus(a_logit.astype(f32) + lw['dt_bias'].astype(f32))
    o_core, S_new = gated_delta_rule(gq, gk, gv, g, beta, s_past.astype(f32))
    o_b = rmsnorm(o_core, lw['g_gdn_out']) * jax.nn.silu(z.reshape(B, L, GDN_HEADS, GDN_DV).astype(f32))
    o_b = o_b.astype(x.dtype).reshape(B, L, GDN_OUT)

    x = x + jnp.concatenate([o_a, o_b], axis=-1) @ lw['w_out']

    h2 = rmsnorm(x, lw['g_ffn_norm'])
    gate = h2 @ lw['w_ffn_gate']
    gate_hist = jnp.concatenate([ffn_past, gate], axis=1)
    gate_c = causal_dwconv(gate_hist, lw['w_ffn_conv']) + lw['b_ffn_conv']
    y = x + (jax.nn.silu(gate_c) * (h2 @ lw['w_ffn_up'])) @ lw['w_ffn_down']

    new_state = (c_kv, k_rope, qkv_hist[:, -(GDN_CONV - 1):], S_new.astype(s_past.dtype),
                 gate_hist[:, -(FFN_CONV - 1):])
    return y, new_state


def setup_inputs(seed: int = 0) -> dict:
    key = jax.random.key(seed)
    ks = jax.random.split(key, 32)
    nrm = lambda k, shape, s: jax.random.normal(k, shape, jnp.float32) * s
    gain = lambda k, n: 1.0 + 0.02 * jax.random.normal(k, (DEPTH, n), jnp.float32)
    dt = jnp.exp(jax.random.uniform(ks[20], (DEPTH, GDN_HEADS), jnp.float32, minval=np.log(1e-3), maxval=np.log(1e-1)))
    return {
        'x_prompt': nrm(ks[0], (BATCH, SEQ, D_MODEL), 1.0),
        'x_sample': nrm(ks[1], (DEC_BATCH, DEC_SEQ, D_MODEL), 1.0),
        'cache_mla_latent': nrm(ks[2], (DEPTH, DEC_BATCH, PAST_LEN, KV_LORA), 1.0),
        'cache_mla_krope': nrm(ks[3], (DEPTH, DEC_BATCH, PAST_LEN, QK_ROPE), 1.0),
        'state_gdn_conv': nrm(ks[4], (DEPTH, DEC_BATCH, GDN_CONV - 1, GDN_CONV_DIM), 1.0),
        'state_gdn_S': nrm(ks[5], (DEPTH, DEC_BATCH, GDN_HEADS, GDN_DK, GDN_DV), 0.1),
        'state_ffn_conv': nrm(ks[6], (DEPTH, DEC_BATCH, FFN_CONV - 1, D_FF), 1.0),
        'g_attn_norm': gain(ks[7], D_MODEL),
        'w_in': nrm(ks[8], (DEPTH, D_MODEL, D_IN), D_MODEL ** -0.5),
        'g_q_lat': gain(ks[9], Q_LORA),
        'g_kv_lat': gain(ks[10], KV_LORA),
        'w_q_up': nrm(ks[11], (DEPTH, Q_LORA, MLA_HEADS * (QK_NOPE + QK_ROPE)), Q_LORA ** -0.5),
        'w_kv_up': nrm(ks[12], (DEPTH, KV_LORA, MLA_HEADS, QK_NOPE + V_HEAD), KV_LORA ** -0.5),
        'g_q_nope': gain(ks[13], QK_NOPE),
        'g_q_rope': gain(ks[14], QK_ROPE),
        'g_k_nope': gain(ks[15], QK_NOPE),
        'g_k_rope': gain(ks[16], QK_ROPE),
        'w_gdn_conv': nrm(ks[17], (DEPTH, GDN_CONV, GDN_CONV_DIM), GDN_CONV ** -0.5),
        'a_log': jnp.log(jax.random.uniform(ks[18], (DEPTH, GDN_HEADS), jnp.float32, minval=1.0, maxval=16.0)),
        'dt_bias': dt + jnp.log(-jnp.expm1(-dt)),
        'g_gdn_out': gain(ks[19], GDN_DV),
        'w_out': nrm(ks[21], (DEPTH, D_MIX, D_MODEL), D_MIX ** -0.5),
        'g_ffn_norm': gain(ks[22], D_MODEL),
        'w_ffn_gate': nrm(ks[23], (DEPTH, D_MODEL, D_FF), D_MODEL ** -0.5),
        'w_ffn_up': nrm(ks[24], (DEPTH, D_MODEL, D_FF), D_MODEL ** -0.5),
        'w_ffn_conv': nrm(ks[25], (DEPTH, FFN_CONV, D_FF), FFN_CONV ** -0.5),
        'b_ffn_conv': nrm(ks[26], (DEPTH, D_FF), 0.02),
        'w_ffn_down': nrm(ks[27], (DEPTH, D_FF, D_MODEL), D_FF ** -0.5),
    }


def reference(x_prompt, x_sample, cache_mla_latent, cache_mla_krope, state_gdn_conv, state_gdn_S, state_ffn_conv,
              g_attn_norm, w_in, g_q_lat, g_kv_lat, w_q_up, w_kv_up, g_q_nope, g_q_rope, g_k_nope, g_k_rope,
              w_gdn_conv, a_log, dt_bias, g_gdn_out, w_out, g_ffn_norm, w_ffn_gate, w_ffn_up, w_ffn_conv,
              b_ffn_conv, w_ffn_down):
    xp, xs = x_prompt, x_sample
    Bp, dtp = x_prompt.shape[0], x_prompt.dtype
    new_p, new_s = [], []
    for l in range(DEPTH):
        lw = dict(g_attn_norm=g_attn_norm[l], w_in=w_in[l], g_q_lat=g_q_lat[l], g_kv_lat=g_kv_lat[l],
                  w_q_up=w_q_up[l], w_kv_up=w_kv_up[l], g_q_nope=g_q_nope[l], g_q_rope=g_q_rope[l],
                  g_k_nope=g_k_nope[l], g_k_rope=g_k_rope[l], w_gdn_conv=w_gdn_conv[l], a_log=a_log[l],
                  dt_bias=dt_bias[l], g_gdn_out=g_gdn_out[l], w_out=w_out[l], g_ffn_norm=g_ffn_norm[l],
                  w_ffn_gate=w_ffn_gate[l], w_ffn_up=w_ffn_up[l], w_ffn_conv=w_ffn_conv[l],
                  b_ffn_conv=b_ffn_conv[l], w_ffn_down=w_ffn_down[l])
        xp, st_p = hybrid_layer(xp,
                                jnp.zeros((Bp, 0, KV_LORA), dtp), jnp.zeros((Bp, 0, QK_ROPE), dtp),
                                jnp.zeros((Bp, GDN_CONV - 1, GDN_CONV_DIM), dtp),
                                jnp.zeros((Bp, GDN_HEADS, GDN_DK, GDN_DV), dtp),
                                jnp.zeros((Bp, FFN_CONV - 1, D_FF), dtp), lw)
        xs, st_s = hybrid_layer(xs, cache_mla_latent[l], cache_mla_krope[l], state_gdn_conv[l],
                                state_gdn_S[l], state_ffn_conv[l], lw)
        new_p.append(st_p)
        new_s.append(st_s)
    p_latent, p_krope, p_gdn_conv, p_gdn_S, p_ffn_conv = [jnp.stack(t) for t in zip(*new_p)]
    s_latent, s_krope, s_gdn_conv, s_gdn_S, s_ffn_conv = [jnp.stack(t) for t in zip(*new_s)]
    return (xp, xs, p_latent, p_krope, p_gdn_conv, p_gdn_S, p_ffn_conv,
            s_latent, s_krope, s_gdn_conv, s_gdn_S, s_ffn_conv)
```

```python
import functools
import math

import jax
import jax.numpy as jnp
import numpy as np
from jax import lax
from jax.experimental import pallas as pl
from jax.experimental.pallas import tpu as pltpu

D_MODEL = 2048
CHUNK = 64
EPS = 1e-6
MLA_HEADS = 8
Q_LORA = 512
KV_LORA = 512
QK_NOPE = 128
QK_ROPE = 64
V_HEAD = 128
ROPE_THETA = 10000.0
GDN_HEADS = 8
GDN_DK = 128
GDN_DV = 128
GDN_CONV = 4
GDN_CONV_DIM = 2 * GDN_HEADS * GDN_DK + GDN_HEADS * GDN_DV
D_FF = 5632
FFN_CONV = 3
MLA_OUT = MLA_HEADS * V_HEAD
GDN_OUT = GDN_HEADS * GDN_DV
QK_DIM = QK_NOPE + QK_ROPE

LANES = 128
SUBLANES = 8
GDN_BLOCK = 128

COL_QKV = 0
COL_Z = COL_QKV + GDN_CONV_DIM
COL_QA = COL_Z + GDN_OUT
COL_KVA = COL_QA + Q_LORA
COL_KR = COL_KVA + KV_LORA
COL_AB = COL_KR + LANES
D_IN_PAD = COL_AB + LANES

BF16 = jnp.bfloat16
F32 = jnp.float32
NT_DIMS = (((1,), (1,)), ((), ()))
TN_DIMS = (((0,), (0,)), ((), ()))


def _params(sem, vmem_mb):
    return pltpu.CompilerParams(dimension_semantics=sem, vmem_limit_bytes=vmem_mb * 1024 * 1024)


def _rms(x, g):
    return x * lax.rsqrt(jnp.mean(x * x, axis=-1, keepdims=True) + EPS) * g


def _sigmoid(x):
    return 1.0 / (1.0 + jnp.exp(-x))


def _softplus(x):
    return jnp.maximum(x, 0.0) + jnp.log(1.0 + jnp.exp(-jnp.abs(x)))


def _norm_matmul_kernel(x_ref, g_ref, w_ref, o_ref, h_ref):
    @pl.when(pl.program_id(1) == 0)
    def _():
        h_ref[...] = _rms(x_ref[...], g_ref[...]).astype(BF16)

    o_ref[...] = jnp.dot(h_ref[...], w_ref[...], preferred_element_type=F32)


def _norm_matmul(x, g, w, tm, tn):
    m, k = x.shape
    n = w.shape[1]
    return pl.pallas_call(
        _norm_matmul_kernel,
        grid=(m // tm, n // tn),
        in_specs=[pl.BlockSpec((tm, k), lambda i, j: (i, 0)),
                  pl.BlockSpec((1, k), lambda i, j: (0, 0)),
                  pl.BlockSpec((k, tn), lambda i, j: (0, j))],
        out_specs=pl.BlockSpec((tm, tn), lambda i, j: (i, j)),
        out_shape=jax.ShapeDtypeStruct((m, n), F32),
        scratch_shapes=[pltpu.VMEM((tm, k), BF16)],
        compiler_params=_params(("arbitrary", "arbitrary"), 48),
        name="in_proj",
    )(x, g, w)


def _rope_pairs(y, cos, sin):
    lane = lax.broadcasted_iota(jnp.int32, (1, LANES), 1)
    first_half = (lane % QK_ROPE) < (QK_ROPE // 2)
    swapped = jnp.where(first_half, pltpu.roll(y, LANES - QK_ROPE // 2, 1), pltpu.roll(y, QK_ROPE // 2, 1))
    return y * cos + swapped * sin


def _mla_pre_kernel(qa_ref, kva_ref, kr_ref, cos_ref, sin_ref, gq_ref, gkv_ref, wq_ref, gqn_ref, gqr_ref,
                    gkr_ref, ckv_ref, krope_ref, qf_ref, *, scale):
    cos = cos_ref[...]
    sin = sin_ref[...]
    lane = lax.broadcasted_iota(jnp.int32, (1, LANES), 1)
    lo = lane < QK_ROPE

    ckv_ref[...] = _rms(kva_ref[...], gkv_ref[...])

    kr = kr_ref[...]
    ss = jnp.sum(jnp.where(lo, kr * kr, 0.0), axis=-1, keepdims=True)
    kr = kr * lax.rsqrt(ss * (1.0 / QK_ROPE) + EPS) * gkr_ref[...]
    krope_ref[...] = _rope_pairs(kr, cos, sin)[:, :QK_ROPE]

    hq = _rms(qa_ref[...], gq_ref[...]).astype(BF16)
    q = jnp.dot(hq, wq_ref[...], preferred_element_type=F32)
    gqn = gqn_ref[...] * scale
    for h in range(MLA_HEADS):
        xn = q[:, h * QK_NOPE:(h + 1) * QK_NOPE]
        qf_ref[0, h, :, 0:QK_NOPE] = (_rms(xn, gqn)).astype(BF16)
    gqr = gqr_ref[...] * scale
    rope0 = MLA_HEADS * QK_NOPE
    for p in range(MLA_HEADS // 2):
        xr = q[:, rope0 + p * LANES: rope0 + (p + 1) * LANES]
        sq = xr * xr
        s_lo = jnp.sum(jnp.where(lo, sq, 0.0), axis=-1, keepdims=True)
        s_hi = jnp.sum(jnp.where(lo, 0.0, sq), axis=-1, keepdims=True)
        r = jnp.where(lo, lax.rsqrt(s_lo * (1.0 / QK_ROPE) + EPS), lax.rsqrt(s_hi * (1.0 / QK_ROPE) + EPS))
        ro = _rope_pairs(xr * r * gqr, cos, sin).astype(BF16)
        qf_ref[0, 2 * p, :, QK_NOPE:QK_DIM] = ro[:, :QK_ROPE]
        qf_ref[0, 2 * p + 1, :, QK_NOPE:QK_DIM] = ro[:, QK_ROPE:]


def _mla_pre(proj, cos, sin, g_q_lat, g_kv_lat, wq, g_q_nope, g_q_rope2, g_k_rope2, nb, lb, tm):
    m = proj.shape[0]
    per_seq = lb // tm
    n_tab = cos.shape[0] // tm
    scale = float(QK_DIM) ** -0.5
    row = lambda i: (i, 0)
    const = lambda i: (0, 0)
    return pl.pallas_call(
        functools.partial(_mla_pre_kernel, scale=scale),
        grid=(m // tm,),
        in_specs=[pl.BlockSpec((tm, Q_LORA), lambda i: (i, COL_QA // Q_LORA)),
                  pl.BlockSpec((tm, KV_LORA), lambda i: (i, COL_KVA // KV_LORA)),
                  pl.BlockSpec((tm, LANES), lambda i: (i, COL_KR // LANES)),
                  pl.BlockSpec((tm, LANES), lambda i: (i % n_tab, 0)),
                  pl.BlockSpec((tm, LANES), lambda i: (i % n_tab, 0)),
                  pl.BlockSpec((1, Q_LORA), const),
                  pl.BlockSpec((1, KV_LORA), const),
                  pl.BlockSpec(wq.shape, const),
                  pl.BlockSpec((1, QK_NOPE), const),
                  pl.BlockSpec((1, LANES), const),
                  pl.BlockSpec((1, LANES), const)],
        out_specs=[pl.BlockSpec((tm, KV_LORA), row),
                   pl.BlockSpec((tm, QK_ROPE), row),
                   pl.BlockSpec((1, MLA_HEADS, tm, QK_DIM), lambda i: (i // per_seq, 0, i % per_seq, 0))],
        out_shape=[jax.ShapeDtypeStruct((m, KV_LORA), F32),
                   jax.ShapeDtypeStruct((m, QK_ROPE), F32),
                   jax.ShapeDtypeStruct((nb, MLA_HEADS, lb, QK_DIM), BF16)],
        compiler_params=_params(("arbitrary",), 40),
        name="mla_pre",
    )(proj, proj, proj, cos, sin, g_q_lat, g_kv_lat, wq, g_q_nope, g_q_rope2, g_k_rope2)


def _kv_up_kernel(lat_ref, kr_ref, wk_ref, wv_ref, gk_ref, kf_ref, v_ref):
    lat = lat_ref[0].astype(BF16)
    kn = jnp.dot(lat, wk_ref[...], preferred_element_type=F32)
    vv = jnp.dot(lat, wv_ref[...], preferred_element_type=F32)
    kr = kr_ref[0].astype(BF16)
    gk = gk_ref[...]
    for h in range(MLA_HEADS):
        kf_ref[0, h, :, 0:QK_NOPE] = _rms(kn[:, h * QK_NOPE:(h + 1) * QK_NOPE], gk).astype(BF16)
        kf_ref[0, h, :, QK_NOPE:QK_DIM] = kr
        v_ref[0, h] = vv[:, h * V_HEAD:(h + 1) * V_HEAD].astype(BF16)


def _kv_up(lat, krope, wk, wv, g_k_nope, tm):
    nb, t, _ = lat.shape
    const = lambda b, i: (0, 0)
    return pl.pallas_call(
        _kv_up_kernel,
        grid=(nb, t // tm),
        in_specs=[pl.BlockSpec((1, tm, KV_LORA), lambda b, i: (b, i, 0)),
                  pl.BlockSpec((1, tm, QK_ROPE), lambda b, i: (b, i, 0)),
                  pl.BlockSpec(wk.shape, const),
                  pl.BlockSpec(wv.shape, const),
                  pl.BlockSpec((1, QK_NOPE), const)],
        out_specs=[pl.BlockSpec((1, MLA_HEADS, tm, QK_DIM), lambda b, i: (b, 0, i, 0)),
                   pl.BlockSpec((1, MLA_HEADS, tm, V_HEAD), lambda b, i: (b, 0, i, 0))],
        out_shape=[jax.ShapeDtypeStruct((nb, MLA_HEADS, t, QK_DIM), BF16),
                   jax.ShapeDtypeStruct((nb, MLA_HEADS, t, V_HEAD), BF16)],
        compiler_params=_params(("arbitrary", "arbitrary"), 40),
        name="kv_up",
    )(lat, krope, wk, wv, g_k_nope)


def _flash_kernel(q_ref, k_ref, v_ref, o_ref, *, tq):
    qi = pl.program_id(2)
    q = q_ref[0, 0]

    def scores(ki):
        k = k_ref[0, 0, pl.ds(pl.multiple_of(ki * tq, tq), tq), :]
        return lax.dot_general(q, k, NT_DIMS, preferred_element_type=F32)

    def update(ki, s, carry):
        m, l, acc = carry
        v = v_ref[0, 0, pl.ds(pl.multiple_of(ki * tq, tq), tq), :]
        m_new = jnp.maximum(m, jnp.max(s, axis=-1, keepdims=True))
        alpha = jnp.exp(m - m_new)
        p = jnp.exp(s - m_new)
        l = alpha * l + jnp.sum(p, axis=-1, keepdims=True)
        acc = alpha * acc + jnp.dot(p.astype(BF16), v, preferred_element_type=F32)
        return m_new, l, acc

    init = (jnp.full((tq, 1), -jnp.inf, F32), jnp.zeros((tq, 1), F32), jnp.zeros((tq, V_HEAD), F32))
    carry = lax.fori_loop(0, qi, lambda ki, c: update(ki, scores(ki), c), init)
    rows = lax.broadcasted_iota(jnp.int32, (tq, tq), 0) // CHUNK
    cols = lax.broadcasted_iota(jnp.int32, (tq, tq), 1) // CHUNK
    s = jnp.where(cols <= rows, scores(qi), -jnp.inf)
    _, l, acc = update(qi, s, carry)
    o_ref[0] = (acc / l).astype(BF16)


def _flash_attention(qf, kf, v, tq):
    nb, nh, lq, _ = qf.shape
    t = kf.shape[2]
    return pl.pallas_call(
        functools.partial(_flash_kernel, tq=tq),
        grid=(nb, nh, lq // tq),
        in_specs=[pl.BlockSpec((1, 1, tq, QK_DIM), lambda b, h, i: (b, h, i, 0)),
                  pl.BlockSpec((1, 1, t, QK_DIM), lambda b, h, i: (b, h, 0, 0)),
                  pl.BlockSpec((1, 1, t, V_HEAD), lambda b, h, i: (b, h, 0, 0))],
        out_specs=pl.BlockSpec((1, tq, V_HEAD), lambda b, h, i: (b, i, h)),
        out_shape=jax.ShapeDtypeStruct((nb, lq, nh * V_HEAD), BF16),
        compiler_params=_params(("arbitrary", "arbitrary", "arbitrary"), 40),
        name="flash_attn",
    )(qf, kf, v)


def _attn_cached_kernel(q_ref, kc_ref, vc_ref, kn_ref, vn_ref, o_ref):
    q = q_ref[0, 0]
    s1 = lax.dot_general(q, kc_ref[0, 0], NT_DIMS, preferred_element_type=F32)
    s2 = lax.dot_general(q, kn_ref[0, 0], NT_DIMS, preferred_element_type=F32)
    m = jnp.maximum(jnp.max(s1, axis=-1, keepdims=True), jnp.max(s2, axis=-1, keepdims=True))
    p1 = jnp.exp(s1 - m)
    p2 = jnp.exp(s2 - m)
    l = jnp.sum(p1, axis=-1, keepdims=True) + jnp.sum(p2, axis=-1, keepdims=True)
    acc = (jnp.dot(p1.astype(BF16), vc_ref[0, 0], preferred_element_type=F32)
           + jnp.dot(p2.astype(BF16), vn_ref[0, 0], preferred_element_type=F32))
    o_ref[...] = (acc / l).astype(BF16)


def _attn_cached(qf, kc, vc, kn, vn, nb, lq):
    nh = qf.shape[1]
    past = kc.shape[2]
    new = lambda b, h: (0, h, b, 0)
    old = lambda b, h: (b, h, 0, 0)
    return pl.pallas_call(
        _attn_cached_kernel,
        grid=(nb, nh),
        in_specs=[pl.BlockSpec((1, 1, lq, QK_DIM), new),
                  pl.BlockSpec((1, 1, past, QK_DIM), old),
                  pl.BlockSpec((1, 1, past, V_HEAD), old),
                  pl.BlockSpec((1, 1, lq, QK_DIM), new),
                  pl.BlockSpec((1, 1, lq, V_HEAD), new)],
        out_specs=pl.BlockSpec((lq, V_HEAD), lambda b, h: (b, h)),
        out_shape=jax.ShapeDtypeStruct((nb * lq, nh * V_HEAD), BF16),
        compiler_params=_params(("arbitrary", "arbitrary"), 40),
        name="attn_cached",
    )(qf, kc, vc, kn, vn)


def _gdn_kernel(qkv_ref, z_ref, abc_ref, abr_ref, tail0_ref, s0_ref, wc_ref, alog_c_ref, dt_c_ref, alog_r_ref,
                dt_r_ref, gout_ref, o_ref, sfin_ref, ext_ref, s_ref, *, blk, valid, n_levels):
    t = pl.program_id(1)
    nt = pl.num_programs(1)
    halo = SUBLANES

    @pl.when(t == 0)
    def _():
        ext_ref[0:halo, :] = tail0_ref[0]
        s_ref[...] = s0_ref[0]

    @pl.when(t > 0)
    def _():
        ext_ref[0:halo, :] = ext_ref[blk:blk + halo, :]

    ext_ref[halo:halo + blk, :] = qkv_ref[0]
    wc = wc_ref[...]
    conv = wc[GDN_CONV - 1:GDN_CONV, :] * ext_ref[halo:halo + blk, :]
    for i in range(1, GDN_CONV):
        conv = conv + wc[GDN_CONV - 1 - i:GDN_CONV - i, :] * ext_ref[halo - i:halo - i + blk, :]
    act = conv * _sigmoid(conv)

    abc = abc_ref[0]
    abr = abr_ref[0, 0]
    rvalid = lax.broadcasted_iota(jnp.int32, (blk, 1), 0) < valid
    cvalid = lax.broadcasted_iota(jnp.int32, (1, blk), 1) < valid
    g_col = jnp.where(rvalid, -jnp.exp(alog_c_ref[...]) * _softplus(abc + dt_c_ref[...]), 0.0)
    beta_col = jnp.where(rvalid, _sigmoid(abc), 0.0)
    g_row = jnp.where(cvalid, -jnp.exp(alog_r_ref[...]) * _softplus(abr + dt_r_ref[...]), 0.0)
    ii = lax.broadcasted_iota(jnp.int32, (blk, blk), 0)
    jj = lax.broadcasted_iota(jnp.int32, (blk, blk), 1)
    incl = ii >= jj
    strict = ii > jj
    gc_col = jnp.dot(incl.astype(F32), g_col, preferred_element_type=F32, precision=lax.Precision.HIGHEST)
    gc_row = jnp.dot(g_row, (ii <= jj).astype(F32), preferred_element_type=F32,
                     precision=lax.Precision.HIGHEST)
    eye = (ii == jj).astype(F32)
    merge_masks = []
    for lvl in range(n_levels):
        half = 1 << lvl
        merge_masks.append((ii // (2 * half) == jj // (2 * half)) & ((ii // half) % 2 == 1)
                           & ((jj // half) % 2 == 0))
    nk = GDN_HEADS * GDN_DK
    gout = gout_ref[...]

    for h in range(GDN_HEADS):
        qh = act[:, h * GDN_DK:(h + 1) * GDN_DK]
        kh = act[:, nk + h * GDN_DK: nk + (h + 1) * GDN_DK]
        vh = act[:, 2 * nk + h * GDN_DV: 2 * nk + (h + 1) * GDN_DV]
        qh = qh * lax.rsqrt(jnp.sum(qh * qh, axis=-1, keepdims=True) + EPS) * (float(GDN_DK) ** -0.5)
        kh = kh * lax.rsqrt(jnp.sum(kh * kh, axis=-1, keepdims=True) + EPS)
        gc = gc_col[:, h:h + 1]
        beta = beta_col[:, GDN_HEADS + h:GDN_HEADS + h + 1]
        decay = jnp.where(incl, jnp.exp(gc - gc_row[h:h + 1, :]), 0.0)
        kb = kh.astype(BF16)
        qb = qh.astype(BF16)
        kk = lax.dot_general(kb, kb, NT_DIMS, preferred_element_type=F32)
        a = beta * kk * decay
        tinv = eye - jnp.where(merge_masks[0], a, 0.0)
        for lvl in range(1, n_levels):
            ab = jnp.where(merge_masks[lvl], a, 0.0).astype(BF16)
            tb = tinv.astype(BF16)
            y = jnp.dot(ab, tb, preferred_element_type=F32)
            tinv = tinv - jnp.dot(tb, y.astype(BF16), preferred_element_type=F32)
        egc = jnp.exp(gc)
        tb = tinv.astype(BF16)
        u = jnp.dot(tb, (vh * beta).astype(BF16), preferred_element_type=F32)
        w = jnp.dot(tb, (kh * (beta * egc)).astype(BF16), preferred_element_type=F32)
        s = s_ref[h]
        sb = s.astype(BF16)
        v_new = u - jnp.dot(w.astype(BF16), sb, preferred_element_type=F32)
        intra = lax.dot_general(qb, kb, NT_DIMS, preferred_element_type=F32) * decay
        vb = v_new.astype(BF16)
        o = (jnp.dot((qh * egc).astype(BF16), sb, preferred_element_type=F32)
             + jnp.dot(intra.astype(BF16), vb, preferred_element_type=F32))
        g_last = gc[blk - 1:blk, :]
        kd = (kh * jnp.exp(g_last - gc)).astype(BF16)
        s_ref[h] = s * jnp.exp(g_last) + lax.dot_general(kd, vb, TN_DIMS, preferred_element_type=F32)
        zh = z_ref[0, :, h * GDN_DV:(h + 1) * GDN_DV]
        o_ref[0, :, h * GDN_DV:(h + 1) * GDN_DV] = (_rms(o, gout) * (zh * _sigmoid(zh))).astype(BF16)

    @pl.when(t == nt - 1)
    def _():
        sfin_ref[0] = s_ref[...]


def _gdn(proj3, ab_rows, tail0, s0, wc, a_log, dt_bias, g_out, valid):
    nb, t, _ = proj3.shape
    blk = GDN_BLOCK
    pad_c = lambda v: jnp.zeros((1, LANES), F32).at[0, :GDN_HEADS].set(v)
    pad_r = lambda v: jnp.zeros((2 * GDN_HEADS, 1), F32).at[:GDN_HEADS, 0].set(v)
    const2 = lambda b, i: (0, 0)
    return pl.pallas_call(
        functools.partial(_gdn_kernel, blk=blk, valid=valid, n_levels=int(math.log2(blk))),
        grid=(nb, t // blk),
        in_specs=[pl.BlockSpec((1, blk, GDN_CONV_DIM), lambda b, i: (b, i, COL_QKV // GDN_CONV_DIM)),
                  pl.BlockSpec((1, blk, GDN_OUT), lambda b, i: (b, i, COL_Z // GDN_OUT)),
                  pl.BlockSpec((1, blk, LANES), lambda b, i: (b, i, COL_AB // LANES)),
                  pl.BlockSpec((1, 1, 2 * GDN_HEADS, blk), lambda b, i: (b, i, 0, 0)),
                  pl.BlockSpec((1, SUBLANES, GDN_CONV_DIM), lambda b, i: (b, 0, 0)),
                  pl.BlockSpec((1, GDN_HEADS, GDN_DK, GDN_DV), lambda b, i: (b, 0, 0, 0)),
                  pl.BlockSpec((GDN_CONV, GDN_CONV_DIM), const2),
                  pl.BlockSpec((1, LANES), const2),
                  pl.BlockSpec((1, LANES), const2),
                  pl.BlockSpec((2 * GDN_HEADS, 1), const2),
                  pl.BlockSpec((2 * GDN_HEADS, 1), const2),
                  pl.BlockSpec((1, GDN_DV), const2)],
        out_specs=[pl.BlockSpec((1, blk, GDN_OUT), lambda b, i: (b, i, 0)),
                   pl.BlockSpec((1, GDN_HEADS, GDN_DK, GDN_DV), lambda b, i: (b, 0, 0, 0))],
        out_shape=[jax.ShapeDtypeStruct((nb, t, GDN_OUT), BF16),
                   jax.ShapeDtypeStruct((nb, GDN_HEADS, GDN_DK, GDN_DV), F32)],
        scratch_shapes=[pltpu.VMEM((blk + 2 * SUBLANES, GDN_CONV_DIM), F32),
                        pltpu.VMEM((GDN_HEADS, GDN_DK, GDN_DV), F32)],
        compiler_params=_params(("arbitrary", "arbitrary"), 40),
        name="gdn",
    )(proj3, proj3, proj3, ab_rows, tail0, s0, wc, pad_c(a_log), pad_c(dt_bias), pad_r(a_log), pad_r(dt_bias),
      g_out)


def _out_proj_kernel(oa_ref, ob_ref, w_ref, x_ref, y_ref):
    y_ref[...] = (x_ref[...]
                  + jnp.dot(oa_ref[...], w_ref[0:MLA_OUT, :], preferred_element_type=F32)
                  + jnp.dot(ob_ref[...], w_ref[MLA_OUT:MLA_OUT + GDN_OUT, :], preferred_element_type=F32))


def _out_proj(o_a, o_b, w, x, tm):
    m = x.shape[0]
    row = lambda i: (i, 0)
    return pl.pallas_call(
        _out_proj_kernel,
        grid=(m // tm,),
        in_specs=[pl.BlockSpec((tm, MLA_OUT), row),
                  pl.BlockSpec((tm, GDN_OUT), row),
                  pl.BlockSpec(w.shape, lambda i: (0, 0)),
                  pl.BlockSpec((tm, D_MODEL), row)],
        out_specs=pl.BlockSpec((tm, D_MODEL), row),
        out_shape=jax.ShapeDtypeStruct((m, D_MODEL), F32),
        compiler_params=_params(("arbitrary",), 48),
        name="out_proj",
    )(o_a, o_b, w, x)


def _ffn_kernel(x_ref, g_ref, halo0_ref, wg_ref, wu_ref, wc_ref, bc_ref, wd_ref, y_ref, st_ref, h_ref, ext_ref,
                carry_ref, acc_ref, *, tm, halo, shift, per_seq):
    i = pl.program_id(0)
    f = pl.program_id(1)

    @pl.when(f == 0)
    def _():
        x = x_ref[...]
        h_ref[...] = _rms(x, g_ref[...]).astype(BF16)
        acc_ref[...] = x

    h = h_ref[...]
    gate = jnp.dot(h, wg_ref[...], preferred_element_type=F32)
    up = jnp.dot(h, wu_ref[...], preferred_element_type=F32)

    @pl.when(i % per_seq == 0)
    def _():
        ext_ref[0:halo, :] = halo0_ref[0]

    @pl.when(i % per_seq != 0)
    def _():
        ext_ref[0:halo, :] = carry_ref[f]

    ext_ref[halo:halo + tm, :] = gate
    last = gate[tm - halo:tm, :]
    carry_ref[f] = last
    st_ref[0] = last
    wc = wc_ref[...]
    gc = (wc[2:3, :] * gate + wc[1:2, :] * ext_ref[halo - shift:halo - shift + tm, :]
          + wc[0:1, :] * ext_ref[halo - 2 * shift:halo - 2 * shift + tm, :] + bc_ref[...])
    act = (gc * _sigmoid(gc)) * up
    acc_ref[...] += jnp.dot(act.astype(BF16), wd_ref[...], preferred_element_type=F32)

    @pl.when(f == pl.num_programs(1) - 1)
    def _():
        y_ref[...] = acc_ref[...]


def _ffn(x, g, halo0, wg, wu, wc, bc, wd, tm, tf, halo, shift, per_seq):
    m = x.shape[0]
    nf = D_FF // tf
    return pl.pallas_call(
        functools.partial(_ffn_kernel, tm=tm, halo=halo, shift=shift, per_seq=per_seq),
        grid=(m // tm, nf),
        in_specs=[pl.BlockSpec((tm, D_MODEL), lambda i, f: (i, 0)),
                  pl.BlockSpec((1, D_MODEL), lambda i, f: (0, 0)),
                  pl.BlockSpec((1, halo, tf), lambda i, f: (i // per_seq, 0, f)),
                  pl.BlockSpec((D_MODEL, tf), lambda i, f: (0, f)),
                  pl.BlockSpec((D_MODEL, tf), lambda i, f: (0, f)),
                  pl.BlockSpec((FFN_CONV, tf), lambda i, f: (0, f)),
                  pl.BlockSpec((1, tf), lambda i, f: (0, f)),
                  pl.BlockSpec((tf, D_MODEL), lambda i, f: (f, 0))],
        out_specs=[pl.BlockSpec((tm, D_MODEL), lambda i, f: (i, 0)),
                   pl.BlockSpec((1, halo, tf), lambda i, f: (i, 0, f))],
        out_shape=[jax.ShapeDtypeStruct((m, D_MODEL), F32),
                   jax.ShapeDtypeStruct((m // tm, halo, D_FF), F32)],
        scratch_shapes=[pltpu.VMEM((tm, D_MODEL), BF16),
                        pltpu.VMEM((halo + tm, tf), F32),
                        pltpu.VMEM((nf, halo, tf), F32),
                        pltpu.VMEM((tm, D_MODEL), F32)],
        compiler_params=_params(("arbitrary", "arbitrary"), 56),
        name="conv_ffn",
    )(x, g, halo0, wg, wu, wc, bc, wd)


def _rope_tables(pos, reps):
    half = QK_ROPE // 2
    inv = 1.0 / (ROPE_THETA ** (jnp.arange(half, dtype=F32) / half))
    ang = pos.astype(F32)[:, None] * inv[None, :]
    cos, sin = jnp.cos(ang), jnp.sin(ang)
    cos = jnp.tile(jnp.concatenate([cos, cos], axis=-1), (reps, LANES // QK_ROPE))
    sin = jnp.tile(jnp.concatenate([-sin, sin], axis=-1), (reps, LANES // QK_ROPE))
    return cos, sin


def _prep_weights(lw):
    w_in = lw['w_in']
    off = np.cumsum([Q_LORA, KV_LORA, QK_ROPE, GDN_CONV_DIM, GDN_OUT, GDN_HEADS, GDN_HEADS]).tolist()
    zc = lambda n: jnp.zeros((D_MODEL, n), w_in.dtype)
    w_in_r = jnp.concatenate([w_in[:, off[2]:off[4]], w_in[:, :off[1]], w_in[:, off[1]:off[2]],
                              zc(LANES - QK_ROPE), w_in[:, off[4]:off[6]], zc(LANES - 2 * GDN_HEADS)], axis=1)
    wq = lw['w_q_up'].reshape(Q_LORA, MLA_HEADS, QK_DIM)
    wq_r = jnp.concatenate([wq[:, :, :QK_NOPE].reshape(Q_LORA, -1), wq[:, :, QK_NOPE:].reshape(Q_LORA, -1)], axis=1)
    wkv = lw['w_kv_up']
    row = lambda v: v.reshape(1, -1).astype(F32)
    return dict(
        w_in=w_in_r.astype(BF16), wq=wq_r.astype(BF16),
        wk=wkv[:, :, :QK_NOPE].reshape(KV_LORA, -1).astype(BF16),
        wv=wkv[:, :, QK_NOPE:].reshape(KV_LORA, -1).astype(BF16),
        w_out=lw['w_out'].astype(BF16), wg=lw['w_ffn_gate'].astype(BF16), wu=lw['w_ffn_up'].astype(BF16),
        wd=lw['w_ffn_down'].astype(BF16),
        g_attn=row(lw['g_attn_norm']), g_q_lat=row(lw['g_q_lat']), g_kv_lat=row(lw['g_kv_lat']),
        g_q_nope=row(lw['g_q_nope']), g_k_nope=row(lw['g_k_nope']),
        g_q_rope2=row(jnp.tile(lw['g_q_rope'], LANES // QK_ROPE)),
        g_k_rope2=row(jnp.tile(lw['g_k_rope'], LANES // QK_ROPE)),
        wc_gdn=lw['w_gdn_conv'].astype(F32), a_log=lw['a_log'].astype(F32), dt_bias=lw['dt_bias'].astype(F32),
        g_gdn_out=row(lw['g_gdn_out']), g_ffn=row(lw['g_ffn_norm']), wc_ffn=lw['w_ffn_conv'].astype(F32),
        bc_ffn=row(lw['b_ffn_conv']))


def _ab_rows(proj3, blk):
    nb, t, _ = proj3.shape
    ab = proj3[:, :, COL_AB:COL_AB + 2 * GDN_HEADS]
    return ab.reshape(nb, t // blk, blk, 2 * GDN_HEADS).transpose(0, 1, 3, 2)


def _pad_rows_front(a, rows):
    return jnp.pad(a, ((0, 0), (rows - a.shape[1], 0), (0, 0)))


def _prompt_layer(x, w):
    nb, lb, _ = x.shape
    m = nb * lb
    xf = x.reshape(m, D_MODEL)
    proj = _norm_matmul(xf, w['g_attn'], w['w_in'], 512, 768)
    cos, sin = _rope_tables(jnp.arange(lb), 1)
    c_kv, k_rope, qf = _mla_pre(proj, cos, sin, w['g_q_lat'], w['g_kv_lat'], w['wq'], w['g_q_nope'],
                                w['g_q_rope2'], w['g_k_rope2'], nb, lb, 512)
    kf, v = _kv_up(c_kv.reshape(nb, lb, KV_LORA), k_rope.reshape(nb, lb, QK_ROPE), w['wk'], w['wv'],
                   w['g_k_nope'], 512)
    o_a = _flash_attention(qf, kf, v, 256).reshape(m, MLA_OUT)

    proj3 = proj.reshape(nb, lb, D_IN_PAD)
    tail0 = jnp.zeros((nb, SUBLANES, GDN_CONV_DIM), F32)
    s0 = jnp.zeros((nb, GDN_HEADS, GDN_DK, GDN_DV), F32)
    o_b, s_new = _gdn(proj3, _ab_rows(proj3, GDN_BLOCK), tail0, s0, w['wc_gdn'], w['a_log'], w['dt_bias'],
                      w['g_gdn_out'], GDN_BLOCK)
    x1 = _out_proj(o_a, o_b.reshape(m, GDN_OUT), w['w_out'], xf, 512)

    tm = 512
    halo0 = jnp.zeros((nb, SUBLANES, D_FF), F32)
    y, gate_tail = _ffn(x1, w['g_ffn'], halo0, w['wg'], w['wu'], w['wc_ffn'], w['bc_ffn'], w['wd'],
                        tm, 512, SUBLANES, 1, lb // tm)
    state = (c_kv.reshape(nb, lb, KV_LORA), k_rope.reshape(nb, lb, QK_ROPE),
             proj3[:, lb - (GDN_CONV - 1):, COL_QKV:COL_QKV + GDN_CONV_DIM], s_new,
             gate_tail.reshape(nb, lb // tm, SUBLANES, D_FF)[:, -1, SUBLANES - (FFN_CONV - 1):, :])
    return y.reshape(nb, lb, D_MODEL), state


def _sample_layer(x, lat_past, krope_past, conv_past, s_past, ffn_past, w):
    nb, lb, _ = x.shape
    past = lat_past.shape[1]
    assert (past + lb - 1) // CHUNK == past // CHUNK and past % CHUNK == 0, "new frames must share one chunk"
    m = nb * lb
    xf = x.reshape(m, D_MODEL)
    proj = _norm_matmul(xf, w['g_attn'], w['w_in'], m, 768)
    cos, sin = _rope_tables(past + jnp.arange(lb), nb)
    c_kv, k_rope, qf = _mla_pre(proj, cos, sin, w['g_q_lat'], w['g_kv_lat'], w['wq'], w['g_q_nope'],
                                w['g_q_rope2'], w['g_k_rope2'], 1, m, m)
    kn, vn = _kv_up(c_kv.reshape(1, m, KV_LORA), k_rope.reshape(1, m, QK_ROPE), w['wk'], w['wv'],
                    w['g_k_nope'], m)
    kc, vc = _kv_up(lat_past, krope_past, w['wk'], w['wv'], w['g_k_nope'], 512)
    o_a = _attn_cached(qf, kc, vc, kn, vn, nb, lb)

    proj3 = proj.reshape(nb, lb, D_IN_PAD)
    proj3p = jnp.pad(proj3, ((0, 0), (0, GDN_BLOCK - lb), (0, 0)))
    tail0 = _pad_rows_front(conv_past.astype(F32), SUBLANES)
    o_b, s_new = _gdn(proj3p, _ab_rows(proj3p, GDN_BLOCK), tail0, s_past.astype(F32), w['wc_gdn'], w['a_log'],
                      w['dt_bias'], w['g_gdn_out'], lb)
    o_b = o_b[:, :lb].reshape(m, GDN_OUT)
    x1 = _out_proj(o_a, o_b, w['w_out'], xf, m)

    x1t = x1.reshape(nb, lb, D_MODEL).transpose(1, 0, 2).reshape(m, D_MODEL)
    n_hist = FFN_CONV - 1
    halo0 = ffn_past.astype(F32).transpose(1, 0, 2).reshape(1, n_hist * nb, D_FF)
    yt, gate_tail = _ffn(x1t, w['g_ffn'], halo0, w['wg'], w['wu'], w['wc_ffn'], w['bc_ffn'], w['wd'],
                         m, 512, n_hist * nb, nb, 1)
    y = yt.reshape(lb, nb, D_MODEL).transpose(1, 0, 2)
    state = (c_kv.reshape(nb, lb, KV_LORA), k_rope.reshape(nb, lb, QK_ROPE),
             proj3[:, lb - (GDN_CONV - 1):, COL_QKV:COL_QKV + GDN_CONV_DIM], s_new,
             gate_tail.reshape(n_hist, nb, D_FF).transpose(1, 0, 2))
    return y, state


def kernel(x_prompt, x_sample, cache_mla_latent, cache_mla_krope, state_gdn_conv, state_gdn_S, state_ffn_conv,
           g_attn_norm, w_in, g_q_lat, g_kv_lat, w_q_up, w_kv_up, g_q_nope, g_q_rope, g_k_nope, g_k_rope,
           w_gdn_conv, a_log, dt_bias, g_gdn_out, w_out, g_ffn_norm, w_ffn_gate, w_ffn_up, w_ffn_conv,
           b_ffn_conv, w_ffn_down):
    xp, xs = x_prompt, x_sample
    new_p, new_s = [], []
    for l in range(w_in.shape[0]):
        w = _prep_weights(dict(
            g_attn_norm=g_attn_norm[l], w_in=w_in[l], g_q_lat=g_q_lat[l], g_kv_lat=g_kv_lat[l], w_q_up=w_q_up[l],
            w_kv_up=w_kv_up[l], g_q_nope=g_q_nope[l], g_q_rope=g_q_rope[l], g_k_nope=g_k_nope[l],
            g_k_rope=g_k_rope[l], w_gdn_conv=w_gdn_conv[l], a_log=a_log[l], dt_bias=dt_bias[l],
            g_gdn_out=g_gdn_out[l], w_out=w_out[l], g_ffn_norm=g_ffn_norm[l], w_ffn_gate=w_ffn_gate[l],
            w_ffn_up=w_ffn_up[l], w_ffn_conv=w_ffn_conv[l], b_ffn_conv=b_ffn_conv[l], w_ffn_down=w_ffn_down[l]))
        xp, st_p = _prompt_layer(xp, w)
        xs, st_s = _sample_layer(xs, cache_mla_latent[l], cache_mla_krope[l], state_gdn_conv[l], state_gdn_S[l],
                                 state_ffn_conv[l], w)
        new_p.append(st_p)
        new_s.append(st_s)
    p_state = [jnp.stack(t) for t in zip(*new_p)]
    s_state = [jnp.stack(t) for t in zip(*new_s)]
    return (xp, xs, *p_state, *s_state)
```

```python
import functools
import math

import jax
import jax.numpy as jnp
import numpy as np
from jax import lax
from jax.experimental import pallas as pl
from jax.experimental.pallas import tpu as pltpu

D_MODEL = 2048
CHUNK = 64
EPS = 1e-6
MLA_HEADS = 8
Q_LORA = 512
KV_LORA = 512
QK_NOPE = 128
QK_ROPE = 64
V_HEAD = 128
ROPE_THETA = 10000.0
GDN_HEADS = 8
GDN_DK = 128
GDN_DV = 128
GDN_CONV = 4
GDN_CONV_DIM = 2 * GDN_HEADS * GDN_DK + GDN_HEADS * GDN_DV
D_FF = 5632
FFN_CONV = 3
MLA_OUT = MLA_HEADS * V_HEAD
GDN_OUT = GDN_HEADS * GDN_DV
QK_DIM = QK_NOPE + QK_ROPE

LANES = 128
SUBLANES = 8
GDN_BLOCK = 128

COL_QKV = 0
COL_Z = COL_QKV + GDN_CONV_DIM
COL_QA = COL_Z + GDN_OUT
COL_KVA = COL_QA + Q_LORA
COL_KR = COL_KVA + KV_LORA
COL_AB = COL_KR + LANES
D_IN_PAD = COL_AB + LANES

BF16 = jnp.bfloat16
F32 = jnp.float32
NT_DIMS = (((1,), (1,)), ((), ()))
TN_DIMS = (((0,), (0,)), ((), ()))


def _params(sem, vmem_mb):
    return pltpu.CompilerParams(dimension_semantics=sem, vmem_limit_bytes=vmem_mb * 1024 * 1024)


def _rms(x, g):
    return x * lax.rsqrt(jnp.mean(x * x, axis=-1, keepdims=True) + EPS) * g


def _sigmoid(x):
    return 1.0 / (1.0 + jnp.exp(-x))


def _softplus(x):
    return jnp.maximum(x, 0.0) + jnp.log(1.0 + jnp.exp(-jnp.abs(x)))


def _norm_matmul_kernel(x_ref, g_ref, w_ref, o_ref, h_ref):
    @pl.when(pl.program_id(1) == 0)
    def _():
        h_ref[...] = _rms(x_ref[...], g_ref[...]).astype(BF16)

    o_ref[...] = jnp.dot(h_ref[...], w_ref[...], preferred_element_type=F32)


def _norm_matmul(x, g, w, tm, tn):
    m, k = x.shape
    n = w.shape[1]
    return pl.pallas_call(
        _norm_matmul_kernel,
        grid=(m // tm, n // tn),
        in_specs=[pl.BlockSpec((tm, k), lambda i, j: (i, 0)),
                  pl.BlockSpec((1, k), lambda i, j: (0, 0)),
                  pl.BlockSpec((k, tn), lambda i, j: (0, j))],
        out_specs=pl.BlockSpec((tm, tn), lambda i, j: (i, j)),
        out_shape=jax.ShapeDtypeStruct((m, n), F32),
        scratch_shapes=[pltpu.VMEM((tm, k), BF16)],
        compiler_params=_params(("arbitrary", "arbitrary"), 48),
        name="in_proj",
    )(x, g, w)


def _rope_pairs(y, cos, sin):
    lane = lax.broadcasted_iota(jnp.int32, (1, LANES), 1)
    first_half = (lane % QK_ROPE) < (QK_ROPE // 2)
    swapped = jnp.where(first_half, pltpu.roll(y, LANES - QK_ROPE // 2, 1), pltpu.roll(y, QK_ROPE // 2, 1))
    return y * cos + swapped * sin


def _mla_pre_kernel(qa_ref, kva_ref, kr_ref, cos_ref, sin_ref, gq_ref, gkv_ref, wq_ref, gqn_ref, gqr_ref,
                    gkr_ref, ckv_ref, krope_ref, qf_ref, *, scale):
    cos = cos_ref[...]
    sin = sin_ref[...]
    lane = lax.broadcasted_iota(jnp.int32, (1, LANES), 1)
    lo = lane < QK_ROPE

    ckv_ref[...] = _rms(kva_ref[...], gkv_ref[...])

    kr = kr_ref[...]
    ss = jnp.sum(jnp.where(lo, kr * kr, 0.0), axis=-1, keepdims=True)
    kr = kr * lax.rsqrt(ss * (1.0 / QK_ROPE) + EPS) * gkr_ref[...]
    krope_ref[...] = _rope_pairs(kr, cos, sin)[:, :QK_ROPE]

    hq = _rms(qa_ref[...], gq_ref[...]).astype(BF16)
    q = jnp.dot(hq, wq_ref[...], preferred_element_type=F32)
    gqn = gqn_ref[...] * scale
    for h in range(MLA_HEADS):
        xn = q[:, h * QK_NOPE:(h + 1) * QK_NOPE]
        qf_ref[0, h, :, 0:QK_NOPE] = (_rms(xn, gqn)).astype(BF16)
    gqr = gqr_ref[...] * scale
    rope0 = MLA_HEADS * QK_NOPE
    for p in range(MLA_HEADS // 2):
        xr = q[:, rope0 + p * LANES: rope0 + (p + 1) * LANES]
        sq = xr * xr
        s_lo = jnp.sum(jnp.where(lo, sq, 0.0), axis=-1, keepdims=True)
        s_hi = jnp.sum(jnp.where(lo, 0.0, sq), axis=-1, keepdims=True)
        r = jnp.where(lo, lax.rsqrt(s_lo * (1.0 / QK_ROPE) + EPS), lax.rsqrt(s_hi * (1.0 / QK_ROPE) + EPS))
        ro = _rope_pairs(xr * r * gqr, cos, sin).astype(BF16)
        qf_ref[0, 2 * p, :, QK_NOPE:QK_DIM] = ro[:, :QK_ROPE]
        qf_ref[0, 2 * p + 1, :, QK_NOPE:QK_DIM] = ro[:, QK_ROPE:]


def _mla_pre(proj, cos, sin, g_q_lat, g_kv_lat, wq, g_q_nope, g_q_rope2, g_k_rope2, nb, lb, tm):
    m = proj.shape[0]
    per_seq = lb // tm
    n_tab = cos.shape[0] // tm
    scale = float(QK_DIM) ** -0.5 * math.log2(math.e)
    row = lambda i: (i, 0)
    const = lambda i: (0, 0)
    return pl.pallas_call(
        functools.partial(_mla_pre_kernel, scale=scale),
        grid=(m // tm,),
        in_specs=[pl.BlockSpec((tm, Q_LORA), lambda i: (i, COL_QA // Q_LORA)),
                  pl.BlockSpec((tm, KV_LORA), lambda i: (i, COL_KVA // KV_LORA)),
                  pl.BlockSpec((tm, LANES), lambda i: (i, COL_KR // LANES)),
                  pl.BlockSpec((tm, LANES), lambda i: (i % n_tab, 0)),
                  pl.BlockSpec((tm, LANES), lambda i: (i % n_tab, 0)),
                  pl.BlockSpec((1, Q_LORA), const),
                  pl.BlockSpec((1, KV_LORA), const),
                  pl.BlockSpec(wq.shape, const),
                  pl.BlockSpec((1, QK_NOPE), const),
                  pl.BlockSpec((1, LANES), const),
                  pl.BlockSpec((1, LANES), const)],
        out_specs=[pl.BlockSpec((tm, KV_LORA), row),
                   pl.BlockSpec((tm, QK_ROPE), row),
                   pl.BlockSpec((1, MLA_HEADS, tm, QK_DIM), lambda i: (i // per_seq, 0, i % per_seq, 0))],
        out_shape=[jax.ShapeDtypeStruct((m, KV_LORA), F32),
                   jax.ShapeDtypeStruct((m, QK_ROPE), F32),
                   jax.ShapeDtypeStruct((nb, MLA_HEADS, lb, QK_DIM), BF16)],
        compiler_params=_params(("arbitrary",), 40),
        name="mla_pre",
    )(proj, proj, proj, cos, sin, g_q_lat, g_kv_lat, wq, g_q_nope, g_q_rope2, g_k_rope2)


def _kv_up_kernel(lat_ref, kr_ref, wk_ref, wv_ref, gk_ref, kf_ref, v_ref):
    lat = lat_ref[0].astype(BF16)
    kn = jnp.dot(lat, wk_ref[...], preferred_element_type=F32)
    vv = jnp.dot(lat, wv_ref[...], preferred_element_type=F32)
    kr = kr_ref[0].astype(BF16)
    gk = gk_ref[...]
    for h in range(MLA_HEADS):
        kf_ref[0, h, :, 0:QK_NOPE] = _rms(kn[:, h * QK_NOPE:(h + 1) * QK_NOPE], gk).astype(BF16)
        kf_ref[0, h, :, QK_NOPE:QK_DIM] = kr
        v_ref[0, h] = vv[:, h * V_HEAD:(h + 1) * V_HEAD].astype(BF16)


def _kv_up(lat, krope, wk, wv, g_k_nope, tm):
    nb, t, _ = lat.shape
    const = lambda b, i: (0, 0)
    return pl.pallas_call(
        _kv_up_kernel,
        grid=(nb, t // tm),
        in_specs=[pl.BlockSpec((1, tm, KV_LORA), lambda b, i: (b, i, 0)),
                  pl.BlockSpec((1, tm, QK_ROPE), lambda b, i: (b, i, 0)),
                  pl.BlockSpec(wk.shape, const),
                  pl.BlockSpec(wv.shape, const),
                  pl.BlockSpec((1, QK_NOPE), const)],
        out_specs=[pl.BlockSpec((1, MLA_HEADS, tm, QK_DIM), lambda b, i: (b, 0, i, 0)),
                   pl.BlockSpec((1, MLA_HEADS, tm, V_HEAD), lambda b, i: (b, 0, i, 0))],
        out_shape=[jax.ShapeDtypeStruct((nb, MLA_HEADS, t, QK_DIM), BF16),
                   jax.ShapeDtypeStruct((nb, MLA_HEADS, t, V_HEAD), BF16)],
        compiler_params=_params(("arbitrary", "arbitrary"), 40),
        name="kv_up",
    )(lat, krope, wk, wv, g_k_nope)


def _flash_kernel(q_ref, k_ref, v_ref, o_ref, m_ref, l_ref, acc_ref, *, tq, tk, sub):
    qi = pl.program_id(2)
    m_ref[...] = jnp.full(m_ref.shape, -jnp.inf, F32)
    l_ref[...] = jnp.zeros(l_ref.shape, F32)
    acc_ref[...] = jnp.zeros(acc_ref.shape, F32)

    def attend_rows(r0, nr, keys, k0, masked):
        rows = pl.ds(r0, nr)
        s = lax.dot_general(q_ref[0, 0, rows, :], k_ref[0, 0, keys, :], NT_DIMS, preferred_element_type=F32)
        if masked:
            rc = (r0 + lax.broadcasted_iota(jnp.int32, (nr, tk), 0)) // CHUNK
            cc = (k0 + lax.broadcasted_iota(jnp.int32, (nr, tk), 1)) // CHUNK
            s = jnp.where(cc <= rc, s, -jnp.inf)
        m = m_ref[rows, :]
        m_new = jnp.maximum(m, jnp.max(s, axis=-1, keepdims=True))
        alpha = jnp.exp2(m - m_new)
        pc = [jnp.exp2(s[:, c * LANES:(c + 1) * LANES] - m_new) for c in range(tk // LANES)]
        psum = pc[0]
        for c in range(1, tk // LANES):
            psum = psum + pc[c]
        p = jnp.concatenate(pc, axis=1)
        m_ref[rows, :] = m_new
        l_ref[rows, :] = alpha * l_ref[rows, :] + psum
        acc_ref[rows, :] = alpha * acc_ref[rows, :] + jnp.dot(p.astype(BF16), v_ref[0, 0, keys, :],
                                                              preferred_element_type=F32)

    def attend(r_lo, r_hi, ki, k0, masked):
        keys = pl.ds(pl.multiple_of(ki * tk, tk), tk)
        for r0 in range(r_lo, r_hi, sub):
            if not masked or r0 + sub > k0:
                attend_rows(r0, sub, keys, k0, masked and r0 < k0 + tk)

    def body(ki, c):
        attend(0, tq, ki, 0, False)
        return c

    n_diag = tq // tk
    lax.fori_loop(0, n_diag * qi, body, 0)
    for j in range(n_diag):
        attend(0, tq, n_diag * qi + j, j * tk, True)
    l = jnp.sum(l_ref[...], axis=-1, keepdims=True)
    o_ref[0] = (acc_ref[...] / l).astype(BF16)


def _flash_attention(qf, kf, v, tq, tk, sub):
    nb, nh, lq, _ = qf.shape
    t = kf.shape[2]
    return pl.pallas_call(
        functools.partial(_flash_kernel, tq=tq, tk=tk, sub=sub),
        grid=(nb, nh, lq // tq),
        in_specs=[pl.BlockSpec((1, 1, tq, QK_DIM), lambda b, h, i: (b, h, i, 0)),
                  pl.BlockSpec((1, 1, t, QK_DIM), lambda b, h, i: (b, h, 0, 0)),
                  pl.BlockSpec((1, 1, t, V_HEAD), lambda b, h, i: (b, h, 0, 0))],
        out_specs=pl.BlockSpec((1, tq, V_HEAD), lambda b, h, i: (b, i, h)),
        out_shape=jax.ShapeDtypeStruct((nb, lq, nh * V_HEAD), BF16),
        scratch_shapes=[pltpu.VMEM((tq, LANES), F32), pltpu.VMEM((tq, LANES), F32),
                        pltpu.VMEM((tq, V_HEAD), F32)],
        compiler_params=_params(("arbitrary", "arbitrary", "arbitrary"), 40),
        name="flash_attn",
    )(qf, kf, v)


def _attn_cached_kernel(qn_ref, qr_ref, lat_ref, kr_ref, latn_ref, krn_ref, wk_ref, wv_ref, gk_ref, o_ref,
                        m_ref, l_ref, acc_ref, *, lq):
    kt = pl.program_id(1)

    @pl.when(kt == 0)
    def _():
        m_ref[...] = jnp.full(m_ref.shape, -jnp.inf, F32)
        l_ref[...] = jnp.zeros(l_ref.shape, F32)
        acc_ref[...] = jnp.zeros(acc_ref.shape, F32)

    def attend(lat, kr):
        latb = lat.astype(BF16)
        kn = jnp.dot(latb, wk_ref[...], preferred_element_type=F32)
        gk = gk_ref[...]
        knb = jnp.concatenate([_rms(kn[:, h * QK_NOPE:(h + 1) * QK_NOPE], gk).astype(BF16)
                               for h in range(MLA_HEADS)], axis=1)
        s = (lax.dot_general(qn_ref[0], knb, NT_DIMS, preferred_element_type=F32)
             + lax.dot_general(qr_ref[0], kr.astype(BF16), NT_DIMS, preferred_element_type=F32))
        m = m_ref[...]
        m_new = jnp.maximum(m, jnp.max(s, axis=-1, keepdims=True))
        alpha = jnp.exp2(m - m_new)
        p = jnp.exp2(s - m_new[:, 0:1])
        m_ref[...] = m_new
        l_ref[...] = alpha * l_ref[...] + jnp.sum(p, axis=-1, keepdims=True)
        acc_ref[...] = alpha[:, 0:1] * acc_ref[...] + jnp.dot(p.astype(BF16), latb, preferred_element_type=F32)

    attend(lat_ref[0], kr_ref[0])

    @pl.when(kt == pl.num_programs(1) - 1)
    def _():
        attend(latn_ref[...], krn_ref[...])
        o_lat = (acc_ref[...] / l_ref[:, 0:1]).astype(BF16)
        for h in range(MLA_HEADS):
            o_ref[:, h * V_HEAD:(h + 1) * V_HEAD] = jnp.dot(
                o_lat[h * lq:(h + 1) * lq, :], wv_ref[:, h * V_HEAD:(h + 1) * V_HEAD],
                preferred_element_type=F32).astype(BF16)


def _attn_cached(qn_bd, qr, lat_past, kr_past, lat_new, kr_new, wk, wv, g_k_nope, lq, tk):
    nb, past, _ = lat_past.shape
    nrow = MLA_HEADS * lq
    per_b = lambda b, k: (b, 0, 0)
    tile = lambda b, k: (b, k, 0)
    new = lambda b, k: (b, 0)
    const = lambda b, k: (0, 0)
    return pl.pallas_call(
        functools.partial(_attn_cached_kernel, lq=lq),
        grid=(nb, past // tk),
        in_specs=[pl.BlockSpec((1, nrow, MLA_HEADS * QK_NOPE), per_b),
                  pl.BlockSpec((1, nrow, QK_ROPE), per_b),
                  pl.BlockSpec((1, tk, KV_LORA), tile),
                  pl.BlockSpec((1, tk, QK_ROPE), tile),
                  pl.BlockSpec((lq, KV_LORA), new),
                  pl.BlockSpec((lq, QK_ROPE), new),
                  pl.BlockSpec(wk.shape, const),
                  pl.BlockSpec(wv.shape, const),
                  pl.BlockSpec((1, QK_NOPE), const)],
        out_specs=pl.BlockSpec((lq, MLA_HEADS * V_HEAD), new),
        out_shape=jax.ShapeDtypeStruct((nb * lq, MLA_HEADS * V_HEAD), BF16),
        scratch_shapes=[pltpu.VMEM((nrow, LANES), F32), pltpu.VMEM((nrow, LANES), F32),
                        pltpu.VMEM((nrow, KV_LORA), F32)],
        compiler_params=_params(("arbitrary", "arbitrary"), 40),
        name="attn_cached",
    )(qn_bd, qr, lat_past, kr_past, lat_new, kr_new, wk, wv, g_k_nope)


def _gdn_kernel(qkv_ref, z_ref, abc_ref, abr_ref, tail0_ref, s0_ref, wc_ref, alog_c_ref, dt_c_ref, alog_r_ref,
                dt_r_ref, gout_ref, o_ref, sfin_ref, ext_ref, s_ref, *, blk, valid, n_levels):
    t = pl.program_id(1)
    nt = pl.num_programs(1)
    halo = SUBLANES

    @pl.when(t == 0)
    def _():
        ext_ref[0:halo, :] = tail0_ref[0]
        s_ref[...] = s0_ref[0]

    @pl.when(t > 0)
    def _():
        ext_ref[0:halo, :] = ext_ref[blk:blk + halo, :]

    pad_rows = lambda x: x if valid == blk else jnp.concatenate(
        [x, jnp.zeros((blk - valid, x.shape[1]), x.dtype)], axis=0)
    ext_ref[halo:halo + blk, :] = pad_rows(qkv_ref[0])
    wc = wc_ref[...]
    conv = wc[GDN_CONV - 1:GDN_CONV, :] * ext_ref[halo:halo + blk, :]
    for i in range(1, GDN_CONV):
        conv = conv + wc[GDN_CONV - 1 - i:GDN_CONV - i, :] * ext_ref[halo - i:halo - i + blk, :]
    act = conv * _sigmoid(conv)

    abc = pad_rows(abc_ref[0])
    abr = abr_ref[0, 0]
    rvalid = lax.broadcasted_iota(jnp.int32, (blk, 1), 0) < valid
    cvalid = lax.broadcasted_iota(jnp.int32, (1, blk), 1) < valid
    g_col = jnp.where(rvalid, -jnp.exp(alog_c_ref[...]) * _softplus(abc + dt_c_ref[...]), 0.0)
    beta_col = jnp.where(rvalid, _sigmoid(abc), 0.0)
    g_row = jnp.where(cvalid, -jnp.exp(alog_r_ref[...]) * _softplus(abr + dt_r_ref[...]), 0.0)
    ii = lax.broadcasted_iota(jnp.int32, (blk, blk), 0)
    jj = lax.broadcasted_iota(jnp.int32, (blk, blk), 1)
    incl = ii >= jj
    gc_col = jnp.dot(incl.astype(F32), g_col, preferred_element_type=F32, precision=lax.Precision.HIGHEST)
    gc_row = jnp.dot(g_row, (ii <= jj).astype(F32), preferred_element_type=F32,
                     precision=lax.Precision.HIGHEST)
    eye = (ii == jj).astype(F32)
    merge_masks = []
    for lvl in range(n_levels):
        half = 1 << lvl
        merge_masks.append((ii // (2 * half) == jj // (2 * half)) & ((ii // half) % 2 == 1)
                           & ((jj // half) % 2 == 0))
    nk = GDN_HEADS * GDN_DK
    gout = gout_ref[...]

    heads = range(GDN_HEADS)
    s_old = [s_ref[h] for h in heads]
    zs = [z_ref[0, :, h * GDN_DV:(h + 1) * GDN_DV] for h in heads]
    q, k, v, gc, beta, decay, a = [], [], [], [], [], [], []
    for h in heads:
        qh = act[:, h * GDN_DK:(h + 1) * GDN_DK]
        kh = act[:, nk + h * GDN_DK: nk + (h + 1) * GDN_DK]
        q.append(qh * lax.rsqrt(jnp.sum(qh * qh, axis=-1, keepdims=True) + EPS) * (float(GDN_DK) ** -0.5))
        k.append(kh * lax.rsqrt(jnp.sum(kh * kh, axis=-1, keepdims=True) + EPS))
        v.append(act[:, 2 * nk + h * GDN_DV: 2 * nk + (h + 1) * GDN_DV])
        gc.append(gc_col[:, h:h + 1])
        beta.append(beta_col[:, GDN_HEADS + h:GDN_HEADS + h + 1])
        decay.append(jnp.where(incl, jnp.exp(gc[h] - gc_row[h:h + 1, :]), 0.0))
    kb = [x.astype(BF16) for x in k]
    kq = [lax.dot_general(jnp.concatenate([kb[h], q[h].astype(BF16)], axis=0), kb[h], NT_DIMS,
                          preferred_element_type=F32) for h in heads]
    a = [beta[h] * kq[h][0:blk] * decay[h] for h in heads]
    tinv = [eye - jnp.where(merge_masks[0], a[h], 0.0) for h in heads]
    for lvl in range(1, n_levels):
        tb = [x.astype(BF16) for x in tinv]
        y = [jnp.dot(jnp.where(merge_masks[lvl], a[h], 0.0).astype(BF16), tb[h], preferred_element_type=F32)
             for h in heads]
        tinv = [tinv[h] - jnp.dot(tb[h], y[h].astype(BF16), preferred_element_type=F32) for h in heads]
    egc = [jnp.exp(gc[h]) for h in heads]
    uw = [jnp.dot(tinv[h].astype(BF16),
                  jnp.concatenate([v[h] * beta[h], k[h] * (beta[h] * egc[h])], axis=1).astype(BF16),
                  preferred_element_type=F32) for h in heads]
    sb = [x.astype(BF16) for x in s_old]
    ws = [jnp.dot(jnp.concatenate([uw[h][:, GDN_DV:], q[h] * egc[h]], axis=0).astype(BF16), sb[h],
                  preferred_element_type=F32) for h in heads]
    vb = [(uw[h][:, 0:GDN_DV] - ws[h][0:blk]).astype(BF16) for h in heads]
    g_last = [gc[h][blk - 1:blk, :] for h in heads]
    o = [ws[h][blk:2 * blk] + jnp.dot((kq[h][blk:2 * blk] * decay[h]).astype(BF16), vb[h],
                                      preferred_element_type=F32) for h in heads]
    s_new = [s_old[h] * jnp.exp(g_last[h])
             + lax.dot_general((k[h] * jnp.exp(g_last[h] - gc[h])).astype(BF16), vb[h], TN_DIMS,
                               preferred_element_type=F32) for h in heads]
    for h in heads:
        s_ref[h] = s_new[h]
        o_ref[0, :, h * GDN_DV:(h + 1) * GDN_DV] = (_rms(o[h][0:valid], gout)
                                                    * (zs[h] * _sigmoid(zs[h]))).astype(BF16)

    @pl.when(t == nt - 1)
    def _():
        sfin_ref[0] = s_ref[...]


def _gdn(proj3, tail0, s0, wc, a_log, dt_bias, g_out):
    nb, t, _ = proj3.shape
    blk = GDN_BLOCK
    valid = min(t, blk)
    ab = proj3[:, :, COL_AB:COL_AB + 2 * GDN_HEADS].reshape(nb, t // valid, valid, 2 * GDN_HEADS)
    ab_rows = jnp.pad(ab.transpose(0, 1, 3, 2), ((0, 0), (0, 0), (0, 0), (0, blk - valid)))
    pad_c = lambda v: jnp.zeros((1, LANES), F32).at[0, :GDN_HEADS].set(v)
    pad_r = lambda v: jnp.zeros((2 * GDN_HEADS, 1), F32).at[:GDN_HEADS, 0].set(v)
    const2 = lambda b, i: (0, 0)
    return pl.pallas_call(
        functools.partial(_gdn_kernel, blk=blk, valid=valid, n_levels=int(math.log2(blk))),
        grid=(nb, t // valid),
        in_specs=[pl.BlockSpec((1, valid, GDN_CONV_DIM), lambda b, i: (b, i, COL_QKV // GDN_CONV_DIM)),
                  pl.BlockSpec((1, valid, GDN_OUT), lambda b, i: (b, i, COL_Z // GDN_OUT)),
                  pl.BlockSpec((1, valid, LANES), lambda b, i: (b, i, COL_AB // LANES)),
                  pl.BlockSpec((1, 1, 2 * GDN_HEADS, blk), lambda b, i: (b, i, 0, 0)),
                  pl.BlockSpec((1, SUBLANES, GDN_CONV_DIM), lambda b, i: (b, 0, 0)),
                  pl.BlockSpec((1, GDN_HEADS, GDN_DK, GDN_DV), lambda b, i: (b, 0, 0, 0)),
                  pl.BlockSpec((GDN_CONV, GDN_CONV_DIM), const2),
                  pl.BlockSpec((1, LANES), const2),
                  pl.BlockSpec((1, LANES), const2),
                  pl.BlockSpec((2 * GDN_HEADS, 1), const2),
                  pl.BlockSpec((2 * GDN_HEADS, 1), const2),
                  pl.BlockSpec((1, GDN_DV), const2)],
        out_specs=[pl.BlockSpec((1, valid, GDN_OUT), lambda b, i: (b, i, 0)),
                   pl.BlockSpec((1, GDN_HEADS, GDN_DK, GDN_DV), lambda b, i: (b, 0, 0, 0))],
        out_shape=[jax.ShapeDtypeStruct((nb, t, GDN_OUT), BF16),
                   jax.ShapeDtypeStruct((nb, GDN_HEADS, GDN_DK, GDN_DV), F32)],
        scratch_shapes=[pltpu.VMEM((blk + 2 * SUBLANES, GDN_CONV_DIM), F32),
                        pltpu.VMEM((GDN_HEADS, GDN_DK, GDN_DV), F32)],
        compiler_params=_params(("arbitrary", "arbitrary"), 40),
        name="gdn",
    )(proj3, proj3, proj3, ab_rows, tail0, s0, wc, pad_c(a_log), pad_c(dt_bias), pad_r(a_log), pad_r(dt_bias),
      g_out)


def _out_proj_kernel(oa_ref, ob_ref, w_ref, x_ref, y_ref):
    y_ref[...] = (x_ref[...]
                  + jnp.dot(oa_ref[...], w_ref[0:MLA_OUT, :], preferred_element_type=F32)
                  + jnp.dot(ob_ref[...], w_ref[MLA_OUT:MLA_OUT + GDN_OUT, :], preferred_element_type=F32))


def _out_proj(o_a, o_b, w, x, tm):
    m = x.shape[0]
    row = lambda i: (i, 0)
    return pl.pallas_call(
        _out_proj_kernel,
        grid=(m // tm,),
        in_specs=[pl.BlockSpec((tm, MLA_OUT), row),
                  pl.BlockSpec((tm, GDN_OUT), row),
                  pl.BlockSpec(w.shape, lambda i: (0, 0)),
                  pl.BlockSpec((tm, D_MODEL), row)],
        out_specs=pl.BlockSpec((tm, D_MODEL), row),
        out_shape=jax.ShapeDtypeStruct((m, D_MODEL), F32),
        compiler_params=_params(("arbitrary",), 48),
        name="out_proj",
    )(o_a, o_b, w, x)


def _ffn_kernel(x_ref, g_ref, halo0_ref, wg_ref, wu_ref, wc_ref, bc_ref, wd_ref, y_ref, st_ref, h_ref, ext_ref,
                carry_ref, *, tm, rows, halo, shift, per_seq):
    i = pl.program_id(0)
    f = pl.program_id(1)

    @pl.when(f == 0)
    def _():
        x = x_ref[...]
        h_ref[...] = _rms(x, g_ref[...]).astype(BF16)
        y_ref[...] = x

    @pl.when(i % per_seq == 0)
    def _():
        ext_ref[0:halo, :] = halo0_ref[0]

    @pl.when(i % per_seq != 0)
    def _():
        ext_ref[0:halo, :] = carry_ref[f]

    wc = wc_ref[...]
    bc = bc_ref[...]
    for r0 in range(0, tm, rows):
        h = h_ref[r0:r0 + rows, :]
        gate = jnp.dot(h, wg_ref[...], preferred_element_type=F32)
        up = jnp.dot(h, wu_ref[...], preferred_element_type=F32)
        ext_ref[halo + r0:halo + r0 + rows, :] = gate
        gc = (wc[2:3, :] * gate + wc[1:2, :] * ext_ref[halo + r0 - shift:halo + r0 - shift + rows, :]
              + wc[0:1, :] * ext_ref[halo + r0 - 2 * shift:halo + r0 - 2 * shift + rows, :] + bc)
        act = (gc * _sigmoid(gc)) * up
        y_ref[r0:r0 + rows, :] += jnp.dot(act.astype(BF16), wd_ref[...], preferred_element_type=F32)
    last = ext_ref[tm:tm + halo, :]
    carry_ref[f] = last
    st_ref[0] = last


def _ffn(x, g, halo0, wg, wu, wc, bc, wd, tm, tf, halo, shift, per_seq):
    m = x.shape[0]
    nf = D_FF // tf
    return pl.pallas_call(
        functools.partial(_ffn_kernel, tm=tm, rows=min(tm, 512), halo=halo, shift=shift, per_seq=per_seq),
        grid=(m // tm, nf),
        in_specs=[pl.BlockSpec((tm, D_MODEL), lambda i, f: (i, 0), pipeline_mode=pl.Buffered(1)),
                  pl.BlockSpec((1, D_MODEL), lambda i, f: (0, 0)),
                  pl.BlockSpec((1, halo, tf), lambda i, f: (i // per_seq, 0, f)),
                  pl.BlockSpec((D_MODEL, tf), lambda i, f: (0, f)),
                  pl.BlockSpec((D_MODEL, tf), lambda i, f: (0, f)),
                  pl.BlockSpec((FFN_CONV, tf), lambda i, f: (0, f)),
                  pl.BlockSpec((1, tf), lambda i, f: (0, f)),
                  pl.BlockSpec((tf, D_MODEL), lambda i, f: (f, 0))],
        out_specs=[pl.BlockSpec((tm, D_MODEL), lambda i, f: (i, 0)),
                   pl.BlockSpec((1, halo, tf), lambda i, f: (i, 0, f))],
        out_shape=[jax.ShapeDtypeStruct((m, D_MODEL), F32),
                   jax.ShapeDtypeStruct((m // tm, halo, D_FF), F32)],
        scratch_shapes=[pltpu.VMEM((tm, D_MODEL), BF16),
                        pltpu.VMEM((halo + tm, tf), F32),
                        pltpu.VMEM((nf, halo, tf), F32)],
        compiler_params=_params(("arbitrary", "arbitrary"), 60),
        name="conv_ffn",
    )(x, g, halo0, wg, wu, wc, bc, wd)


def _rope_tables(pos, reps):
    half = QK_ROPE // 2
    inv = 1.0 / (ROPE_THETA ** (jnp.arange(half, dtype=F32) / half))
    ang = pos.astype(F32)[:, None] * inv[None, :]
    cos, sin = jnp.cos(ang), jnp.sin(ang)
    cos = jnp.tile(jnp.concatenate([cos, cos], axis=-1), (reps, LANES // QK_ROPE))
    sin = jnp.tile(jnp.concatenate([-sin, sin], axis=-1), (reps, LANES // QK_ROPE))
    return cos, sin


def _prep_weights(lw):
    w_in = lw['w_in']
    off = np.cumsum([Q_LORA, KV_LORA, QK_ROPE, GDN_CONV_DIM, GDN_OUT, GDN_HEADS, GDN_HEADS]).tolist()
    zc = lambda n: jnp.zeros((D_MODEL, n), w_in.dtype)
    w_in_r = jnp.concatenate([w_in[:, off[2]:off[4]], w_in[:, :off[1]], w_in[:, off[1]:off[2]],
                              zc(LANES - QK_ROPE), w_in[:, off[4]:off[6]], zc(LANES - 2 * GDN_HEADS)], axis=1)
    wq = lw['w_q_up'].reshape(Q_LORA, MLA_HEADS, QK_DIM)
    wq_r = jnp.concatenate([wq[:, :, :QK_NOPE].reshape(Q_LORA, -1), wq[:, :, QK_NOPE:].reshape(Q_LORA, -1)], axis=1)
    wkv = lw['w_kv_up']
    row = lambda v: v.reshape(1, -1).astype(F32)
    return dict(
        w_in=w_in_r.astype(BF16), wq=wq_r.astype(BF16),
        wk=wkv[:, :, :QK_NOPE].reshape(KV_LORA, -1).astype(BF16),
        wv=wkv[:, :, QK_NOPE:].reshape(KV_LORA, -1).astype(BF16),
        w_out=lw['w_out'].astype(BF16), wg=lw['w_ffn_gate'].astype(BF16), wu=lw['w_ffn_up'].astype(BF16),
        wd=lw['w_ffn_down'].astype(BF16),
        g_attn=row(lw['g_attn_norm']), g_q_lat=row(lw['g_q_lat']), g_kv_lat=row(lw['g_kv_lat']),
        g_q_nope=row(lw['g_q_nope']), g_k_nope=row(lw['g_k_nope']),
        g_q_rope2=row(jnp.tile(lw['g_q_rope'], LANES // QK_ROPE)),
        g_k_rope2=row(jnp.tile(lw['g_k_rope'], LANES // QK_ROPE)),
        wc_gdn=lw['w_gdn_conv'].astype(F32), a_log=lw['a_log'].astype(F32), dt_bias=lw['dt_bias'].astype(F32),
        g_gdn_out=row(lw['g_gdn_out']), g_ffn=row(lw['g_ffn_norm']), wc_ffn=lw['w_ffn_conv'].astype(F32),
        bc_ffn=row(lw['b_ffn_conv']))


def _pad_rows_front(a, rows):
    return jnp.pad(a, ((0, 0), (rows - a.shape[1], 0), (0, 0)))


def _prompt_layer(x, w):
    nb, lb, _ = x.shape
    m = nb * lb
    xf = x.reshape(m, D_MODEL)
    proj = _norm_matmul(xf, w['g_attn'], w['w_in'], 1024, 768)
    cos, sin = _rope_tables(jnp.arange(lb), 1)
    c_kv, k_rope, qf = _mla_pre(proj, cos, sin, w['g_q_lat'], w['g_kv_lat'], w['wq'], w['g_q_nope'],
                                w['g_q_rope2'], w['g_k_rope2'], nb, lb, 512)
    kf, v = _kv_up(c_kv.reshape(nb, lb, KV_LORA), k_rope.reshape(nb, lb, QK_ROPE), w['wk'], w['wv'],
                   w['g_k_nope'], 512)
    o_a = _flash_attention(qf, kf, v, 2048, 1024, 1024).reshape(m, MLA_OUT)

    proj3 = proj.reshape(nb, lb, D_IN_PAD)
    tail0 = jnp.zeros((nb, SUBLANES, GDN_CONV_DIM), F32)
    s0 = jnp.zeros((nb, GDN_HEADS, GDN_DK, GDN_DV), F32)
    o_b, s_new = _gdn(proj3, tail0, s0, w['wc_gdn'], w['a_log'], w['dt_bias'], w['g_gdn_out'])
    x1 = _out_proj(o_a, o_b.reshape(m, GDN_OUT), w['w_out'], xf, 512)

    tm = 1024
    halo0 = jnp.zeros((nb, SUBLANES, D_FF), F32)
    y, gate_tail = _ffn(x1, w['g_ffn'], halo0, w['wg'], w['wu'], w['wc_ffn'], w['bc_ffn'], w['wd'],
                        tm, 512, SUBLANES, 1, lb // tm)
    state = (c_kv.reshape(nb, lb, KV_LORA), k_rope.reshape(nb, lb, QK_ROPE),
             proj3[:, lb - (GDN_CONV - 1):, COL_QKV:COL_QKV + GDN_CONV_DIM], s_new,
             gate_tail.reshape(nb, lb // tm, SUBLANES, D_FF)[:, -1, SUBLANES - (FFN_CONV - 1):, :])
    return y.reshape(nb, lb, D_MODEL), state


def _sample_layer(x, lat_past, krope_past, conv_past, s_past, ffn_past, w):
    nb, lb, _ = x.shape
    past = lat_past.shape[1]
    assert (past + lb - 1) // CHUNK == past // CHUNK and past % CHUNK == 0, "new frames must share one chunk"
    m = nb * lb
    xf = x.reshape(m, D_MODEL)
    proj = _norm_matmul(xf, w['g_attn'], w['w_in'], m, 768)
    cos, sin = _rope_tables(past + jnp.arange(lb), nb)
    c_kv, k_rope, qf = _mla_pre(proj, cos, sin, w['g_q_lat'], w['g_kv_lat'], w['wq'], w['g_q_nope'],
                                w['g_q_rope2'], w['g_k_rope2'], 1, m, m)
    qh = qf[0].reshape(MLA_HEADS, nb, lb, QK_DIM).transpose(1, 0, 2, 3)
    qn_bd = jnp.einsum('bhqd,hg->bhqgd', qh[..., :QK_NOPE], jnp.eye(MLA_HEADS, dtype=qh.dtype))
    qn_bd = qn_bd.reshape(nb, MLA_HEADS * lb, MLA_HEADS * QK_NOPE)
    qr = qh[..., QK_NOPE:].reshape(nb, MLA_HEADS * lb, QK_ROPE)
    o_a = _attn_cached(qn_bd, qr, lat_past, krope_past, c_kv, k_rope, w['wk'], w['wv'], w['g_k_nope'], lb, 512)

    proj3 = proj.reshape(nb, lb, D_IN_PAD)
    tail0 = _pad_rows_front(conv_past.astype(F32), SUBLANES)
    o_b, s_new = _gdn(proj3, tail0, s_past.astype(F32), w['wc_gdn'], w['a_log'], w['dt_bias'], w['g_gdn_out'])
    x1 = _out_proj(o_a, o_b.reshape(m, GDN_OUT), w['w_out'], xf, m)

    x1t = x1.reshape(nb, lb, D_MODEL).transpose(1, 0, 2).reshape(m, D_MODEL)
    n_hist = FFN_CONV - 1
    halo0 = ffn_past.astype(F32).transpose(1, 0, 2).reshape(1, n_hist * nb, D_FF)
    yt, gate_tail = _ffn(x1t, w['g_ffn'], halo0, w['wg'], w['wu'], w['wc_ffn'], w['bc_ffn'], w['wd'],
                         m, 512, n_hist * nb, nb, 1)
    y = yt.reshape(lb, nb, D_MODEL).transpose(1, 0, 2)
    state = (c_kv.reshape(nb, lb, KV_LORA), k_rope.reshape(nb, lb, QK_ROPE),
             proj3[:, lb - (GDN_CONV - 1):, COL_QKV:COL_QKV + GDN_CONV_DIM], s_new,
             gate_tail.reshape(n_hist, nb, D_FF).transpose(1, 0, 2))
    return y, state


def kernel(x_prompt, x_sample, cache_mla_latent, cache_mla_krope, state_gdn_conv, state_gdn_S, state_ffn_conv,
           g_attn_norm, w_in, g_q_lat, g_kv_lat, w_q_up, w_kv_up, g_q_nope, g_q_rope, g_k_nope, g_k_rope,
           w_gdn_conv, a_log, dt_bias, g_gdn_out, w_out, g_ffn_norm, w_ffn_gate, w_ffn_up, w_ffn_conv,
           b_ffn_conv, w_ffn_down):
    xp, xs = x_prompt, x_sample
    new_p, new_s = [], []
    for l in range(w_in.shape[0]):
        w = _prep_weights(dict(
            g_attn_norm=g_attn_norm[l], w_in=w_in[l], g_q_lat=g_q_lat[l], g_kv_lat=g_kv_lat[l], w_q_up=w_q_up[l],
            w_kv_up=w_kv_up[l], g_q_nope=g_q_nope[l], g_q_rope=g_q_rope[l], g_k_nope=g_k_nope[l],
            g_k_rope=g_k_rope[l], w_gdn_conv=w_gdn_conv[l], a_log=a_log[l], dt_bias=dt_bias[l],
            g_gdn_out=g_gdn_out[l], w_out=w_out[l], g_ffn_norm=g_ffn_norm[l], w_ffn_gate=w_ffn_gate[l],
            w_ffn_up=w_ffn_up[l], w_ffn_conv=w_ffn_conv[l], b_ffn_conv=b_ffn_conv[l], w_ffn_down=w_ffn_down[l]))
        xp, st_p = _prompt_layer(xp, w)
        xs, st_s = _sample_layer(xs, cache_mla_latent[l], cache_mla_krope[l], state_gdn_conv[l], state_gdn_S[l],
                                 state_ffn_conv[l], w)
        new_p.append(st_p)
        new_s.append(st_s)
    p_state = [jnp.stack(t) for t in zip(*new_p)]
    s_state = [jnp.stack(t) for t in zip(*new_s)]
    return (xp, xs, *p_state, *s_state)
```

```python
import functools
import math

import jax
import jax.numpy as jnp
import numpy as np
from jax import lax
from jax.experimental import pallas as pl
from jax.experimental.pallas import tpu as pltpu

D_MODEL = 2048
CHUNK = 64
EPS = 1e-6
MLA_HEADS = 8
Q_LORA = 512
KV_LORA = 512
QK_NOPE = 128
QK_ROPE = 64
V_HEAD = 128
ROPE_THETA = 10000.0
GDN_HEADS = 8
GDN_DK = 128
GDN_DV = 128
GDN_CONV = 4
GDN_CONV_DIM = 2 * GDN_HEADS * GDN_DK + GDN_HEADS * GDN_DV
D_FF = 5632
FFN_CONV = 3
MLA_OUT = MLA_HEADS * V_HEAD
GDN_OUT = GDN_HEADS * GDN_DV
QK_DIM = QK_NOPE + QK_ROPE

LANES = 128
SUBLANES = 8
GDN_BLOCK = 128
GDN_BLOCKS_PER_STEP = 2

COL_QKV = 0
COL_Z = COL_QKV + GDN_CONV_DIM
COL_QA = COL_Z + GDN_OUT
COL_KVA = COL_QA + Q_LORA
COL_KR = COL_KVA + KV_LORA
COL_AB = COL_KR + LANES
D_IN_PAD = COL_AB + LANES

BF16 = jnp.bfloat16
F32 = jnp.float32
NT_DIMS = (((1,), (1,)), ((), ()))
TN_DIMS = (((0,), (0,)), ((), ()))


def _params(sem, vmem_mb):
    return pltpu.CompilerParams(dimension_semantics=sem, vmem_limit_bytes=vmem_mb * 1024 * 1024)


def _rms(x, g):
    return x * lax.rsqrt(jnp.mean(x * x, axis=-1, keepdims=True) + EPS) * g


def _sigmoid(x):
    return 1.0 / (1.0 + jnp.exp(-x))


def _softplus(x):
    return jnp.maximum(x, 0.0) + jnp.log(1.0 + jnp.exp(-jnp.abs(x)))


def _norm_matmul_kernel(x_ref, g_ref, w_ref, o_ref, h_ref):
    @pl.when(pl.program_id(1) == 0)
    def _():
        h_ref[...] = _rms(x_ref[...], g_ref[...]).astype(BF16)

    o_ref[...] = jnp.dot(h_ref[...], w_ref[...], preferred_element_type=F32)


def _norm_matmul(x, g, w, tm, tn):
    m, k = x.shape
    n = w.shape[1]
    return pl.pallas_call(
        _norm_matmul_kernel,
        grid=(m // tm, n // tn),
        in_specs=[pl.BlockSpec((tm, k), lambda i, j: (i, 0)),
                  pl.BlockSpec((1, k), lambda i, j: (0, 0)),
                  pl.BlockSpec((k, tn), lambda i, j: (0, j))],
        out_specs=pl.BlockSpec((tm, tn), lambda i, j: (i, j)),
        out_shape=jax.ShapeDtypeStruct((m, n), F32),
        scratch_shapes=[pltpu.VMEM((tm, k), BF16)],
        compiler_params=_params(("arbitrary", "arbitrary"), 56),
        name="in_proj",
    )(x, g, w)


def _rope_pairs(y, cos, sin):
    lane = lax.broadcasted_iota(jnp.int32, (1, LANES), 1)
    first_half = (lane % QK_ROPE) < (QK_ROPE // 2)
    swapped = jnp.where(first_half, pltpu.roll(y, LANES - QK_ROPE // 2, 1), pltpu.roll(y, QK_ROPE // 2, 1))
    return y * cos + swapped * sin


def _mla_pre_kernel(qa_ref, kva_ref, kr_ref, cos_ref, sin_ref, gq_ref, gkv_ref, wq_ref, gqn_ref, gqr_ref,
                    gkr_ref, ckv_ref, krope_ref, qf_ref, *, scale):
    cos = cos_ref[...]
    sin = sin_ref[...]
    lane = lax.broadcasted_iota(jnp.int32, (1, LANES), 1)
    lo = lane < QK_ROPE

    ckv_ref[...] = _rms(kva_ref[...], gkv_ref[...])

    kr = kr_ref[...]
    ss = jnp.sum(jnp.where(lo, kr * kr, 0.0), axis=-1, keepdims=True)
    kr = kr * lax.rsqrt(ss * (1.0 / QK_ROPE) + EPS) * gkr_ref[...]
    krope_ref[...] = _rope_pairs(kr, cos, sin)[:, :QK_ROPE]

    hq = _rms(qa_ref[...], gq_ref[...]).astype(BF16)
    q = jnp.dot(hq, wq_ref[...], preferred_element_type=F32)
    gqn = gqn_ref[...] * scale
    for h in range(MLA_HEADS):
        xn = q[:, h * QK_NOPE:(h + 1) * QK_NOPE]
        qf_ref[0, h, :, 0:QK_NOPE] = (_rms(xn, gqn)).astype(BF16)
    gqr = gqr_ref[...] * scale
    rope0 = MLA_HEADS * QK_NOPE
    for p in range(MLA_HEADS // 2):
        xr = q[:, rope0 + p * LANES: rope0 + (p + 1) * LANES]
        sq = xr * xr
        s_lo = jnp.sum(jnp.where(lo, sq, 0.0), axis=-1, keepdims=True)
        s_hi = jnp.sum(jnp.where(lo, 0.0, sq), axis=-1, keepdims=True)
        r = jnp.where(lo, lax.rsqrt(s_lo * (1.0 / QK_ROPE) + EPS), lax.rsqrt(s_hi * (1.0 / QK_ROPE) + EPS))
        ro = _rope_pairs(xr * r * gqr, cos, sin).astype(BF16)
        qf_ref[0, 2 * p, :, QK_NOPE:QK_DIM] = ro[:, :QK_ROPE]
        qf_ref[0, 2 * p + 1, :, QK_NOPE:QK_DIM] = ro[:, QK_ROPE:]


def _mla_pre(proj, cos, sin, g_q_lat, g_kv_lat, wq, g_q_nope, g_q_rope2, g_k_rope2, nb, lb, tm):
    m = proj.shape[0]
    per_seq = lb // tm
    n_tab = cos.shape[0] // tm
    scale = float(QK_DIM) ** -0.5 * math.log2(math.e)
    row = lambda i: (i, 0)
    const = lambda i: (0, 0)
    return pl.pallas_call(
        functools.partial(_mla_pre_kernel, scale=scale),
        grid=(m // tm,),
        in_specs=[pl.BlockSpec((tm, Q_LORA), lambda i: (i, COL_QA // Q_LORA)),
                  pl.BlockSpec((tm, KV_LORA), lambda i: (i, COL_KVA // KV_LORA)),
                  pl.BlockSpec((tm, LANES), lambda i: (i, COL_KR // LANES)),
                  pl.BlockSpec((tm, LANES), lambda i: (i % n_tab, 0)),
                  pl.BlockSpec((tm, LANES), lambda i: (i % n_tab, 0)),
                  pl.BlockSpec((1, Q_LORA), const),
                  pl.BlockSpec((1, KV_LORA), const),
                  pl.BlockSpec(wq.shape, const),
                  pl.BlockSpec((1, QK_NOPE), const),
                  pl.BlockSpec((1, LANES), const),
                  pl.BlockSpec((1, LANES), const)],
        out_specs=[pl.BlockSpec((tm, KV_LORA), row),
                   pl.BlockSpec((tm, QK_ROPE), row),
                   pl.BlockSpec((1, MLA_HEADS, tm, QK_DIM), lambda i: (i // per_seq, 0, i % per_seq, 0))],
        out_shape=[jax.ShapeDtypeStruct((m, KV_LORA), F32),
                   jax.ShapeDtypeStruct((m, QK_ROPE), F32),
                   jax.ShapeDtypeStruct((nb, MLA_HEADS, lb, QK_DIM), BF16)],
        compiler_params=_params(("arbitrary",), 40),
        name="mla_pre",
    )(proj, proj, proj, cos, sin, g_q_lat, g_kv_lat, wq, g_q_nope, g_q_rope2, g_k_rope2)


def _kv_up_kernel(lat_ref, kr_ref, wk_ref, wv_ref, gk_ref, kf_ref, v_ref):
    lat = lat_ref[0].astype(BF16)
    kn = jnp.dot(lat, wk_ref[...], preferred_element_type=F32)
    vv = jnp.dot(lat, wv_ref[...], preferred_element_type=F32)
    kr = kr_ref[0].astype(BF16)
    gk = gk_ref[...]
    for h in range(MLA_HEADS):
        kf_ref[0, h, :, 0:QK_NOPE] = _rms(kn[:, h * QK_NOPE:(h + 1) * QK_NOPE], gk).astype(BF16)
        kf_ref[0, h, :, QK_NOPE:QK_DIM] = kr
        v_ref[0, h] = vv[:, h * V_HEAD:(h + 1) * V_HEAD].astype(BF16)


def _kv_up(lat, krope, wk, wv, g_k_nope, tm):
    nb, t, _ = lat.shape
    const = lambda b, i: (0, 0)
    return pl.pallas_call(
        _kv_up_kernel,
        grid=(nb, t // tm),
        in_specs=[pl.BlockSpec((1, tm, KV_LORA), lambda b, i: (b, i, 0)),
                  pl.BlockSpec((1, tm, QK_ROPE), lambda b, i: (b, i, 0)),
                  pl.BlockSpec(wk.shape, const),
                  pl.BlockSpec(wv.shape, const),
                  pl.BlockSpec((1, QK_NOPE), const)],
        out_specs=[pl.BlockSpec((1, MLA_HEADS, tm, QK_DIM), lambda b, i: (b, 0, i, 0)),
                   pl.BlockSpec((1, MLA_HEADS, tm, V_HEAD), lambda b, i: (b, 0, i, 0))],
        out_shape=[jax.ShapeDtypeStruct((nb, MLA_HEADS, t, QK_DIM), BF16),
                   jax.ShapeDtypeStruct((nb, MLA_HEADS, t, V_HEAD), BF16)],
        compiler_params=_params(("arbitrary", "arbitrary"), 40),
        name="kv_up",
    )(lat, krope, wk, wv, g_k_nope)


def _flash_kernel(q_ref, k_ref, v_ref, o_ref, m_ref, l_ref, acc_ref, *, tq, tk, sub):
    qi = pl.program_id(2)
    m_ref[...] = jnp.full(m_ref.shape, -jnp.inf, F32)
    l_ref[...] = jnp.zeros(l_ref.shape, F32)
    acc_ref[...] = jnp.zeros(acc_ref.shape, F32)

    def attend_rows(r0, nr, keys, k0, masked):
        rows = pl.ds(r0, nr)
        s = lax.dot_general(q_ref[0, 0, rows, :], k_ref[0, 0, keys, :], NT_DIMS, preferred_element_type=F32)
        if masked:
            rc = (r0 + lax.broadcasted_iota(jnp.int32, (nr, tk), 0)) // CHUNK
            cc = (k0 + lax.broadcasted_iota(jnp.int32, (nr, tk), 1)) // CHUNK
            s = jnp.where(cc <= rc, s, -jnp.inf)
        m = m_ref[rows, :]
        m_new = jnp.maximum(m, jnp.max(s, axis=-1, keepdims=True))
        alpha = jnp.exp2(m - m_new)
        pc = [jnp.exp2(s[:, c * LANES:(c + 1) * LANES] - m_new) for c in range(tk // LANES)]
        psum = pc[0]
        for c in range(1, tk // LANES):
            psum = psum + pc[c]
        p = jnp.concatenate(pc, axis=1)
        m_ref[rows, :] = m_new
        l_ref[rows, :] = alpha * l_ref[rows, :] + psum
        acc_ref[rows, :] = alpha * acc_ref[rows, :] + jnp.dot(p.astype(BF16), v_ref[0, 0, keys, :],
                                                              preferred_element_type=F32)

    def attend(r_lo, r_hi, ki, k0, masked):
        keys = pl.ds(pl.multiple_of(ki * tk, tk), tk)
        for r0 in range(r_lo, r_hi, sub):
            if not masked or r0 + sub > k0:
                attend_rows(r0, sub, keys, k0, masked and r0 < k0 + tk)

    def body(ki, c):
        attend(0, tq, ki, 0, False)
        return c

    n_diag = tq // tk
    lax.fori_loop(0, n_diag * qi, body, 0)
    for j in range(n_diag):
        attend(0, tq, n_diag * qi + j, j * tk, True)
    l = jnp.sum(l_ref[...], axis=-1, keepdims=True)
    o_ref[0] = (acc_ref[...] / l).astype(BF16)


def _flash_attention(qf, kf, v, tq, tk, sub):
    nb, nh, lq, _ = qf.shape
    t = kf.shape[2]
    return pl.pallas_call(
        functools.partial(_flash_kernel, tq=tq, tk=tk, sub=sub),
        grid=(nb, nh, lq // tq),
        in_specs=[pl.BlockSpec((1, 1, tq, QK_DIM), lambda b, h, i: (b, h, i, 0)),
                  pl.BlockSpec((1, 1, t, QK_DIM), lambda b, h, i: (b, h, 0, 0)),
                  pl.BlockSpec((1, 1, t, V_HEAD), lambda b, h, i: (b, h, 0, 0))],
        out_specs=pl.BlockSpec((1, tq, V_HEAD), lambda b, h, i: (b, i, h)),
        out_shape=jax.ShapeDtypeStruct((nb, lq, nh * V_HEAD), BF16),
        scratch_shapes=[pltpu.VMEM((tq, LANES), F32), pltpu.VMEM((tq, LANES), F32),
                        pltpu.VMEM((tq, V_HEAD), F32)],
        compiler_params=_params(("arbitrary", "arbitrary", "arbitrary"), 40),
        name="flash_attn",
    )(qf, kf, v)


def _attn_cached_kernel(qn_ref, qr_ref, lat_ref, kr_ref, latn_ref, krn_ref, wk_ref, wv_ref, gk_ref, o_ref,
                        m_ref, l_ref, acc_ref, *, lq):
    kt = pl.program_id(1)

    @pl.when(kt == 0)
    def _():
        m_ref[...] = jnp.full(m_ref.shape, -jnp.inf, F32)
        l_ref[...] = jnp.zeros(l_ref.shape, F32)
        acc_ref[...] = jnp.zeros(acc_ref.shape, F32)

    def attend(lat, kr):
        latb = lat.astype(BF16)
        kn = jnp.dot(latb, wk_ref[...], preferred_element_type=F32)
        gk = gk_ref[...]
        knb = jnp.concatenate([_rms(kn[:, h * QK_NOPE:(h + 1) * QK_NOPE], gk).astype(BF16)
                               for h in range(MLA_HEADS)], axis=1)
        s = (lax.dot_general(qn_ref[0], knb, NT_DIMS, preferred_element_type=F32)
             + lax.dot_general(qr_ref[0], kr.astype(BF16), NT_DIMS, preferred_element_type=F32))
        m = m_ref[...]
        m_new = jnp.maximum(m, jnp.max(s, axis=-1, keepdims=True))
        alpha = jnp.exp2(m - m_new)
        p = jnp.exp2(s - m_new[:, 0:1])
        m_ref[...] = m_new
        l_ref[...] = alpha * l_ref[...] + jnp.sum(p, axis=-1, keepdims=True)
        acc_ref[...] = alpha[:, 0:1] * acc_ref[...] + jnp.dot(p.astype(BF16), latb, preferred_element_type=F32)

    attend(lat_ref[0], kr_ref[0])

    @pl.when(kt == pl.num_programs(1) - 1)
    def _():
        attend(latn_ref[...], krn_ref[...])
        o_lat = (acc_ref[...] / l_ref[:, 0:1]).astype(BF16)
        for h in range(MLA_HEADS):
            o_ref[:, h * V_HEAD:(h + 1) * V_HEAD] = jnp.dot(
                o_lat[h * lq:(h + 1) * lq, :], wv_ref[:, h * V_HEAD:(h + 1) * V_HEAD],
                preferred_element_type=F32).astype(BF16)


def _attn_cached(qn_bd, qr, lat_past, kr_past, lat_new, kr_new, wk, wv, g_k_nope, lq, tk):
    nb, past, _ = lat_past.shape
    nrow = MLA_HEADS * lq
    per_b = lambda b, k: (b, 0, 0)
    tile = lambda b, k: (b, k, 0)
    new = lambda b, k: (b, 0)
    const = lambda b, k: (0, 0)
    return pl.pallas_call(
        functools.partial(_attn_cached_kernel, lq=lq),
        grid=(nb, past // tk),
        in_specs=[pl.BlockSpec((1, nrow, MLA_HEADS * QK_NOPE), per_b),
                  pl.BlockSpec((1, nrow, QK_ROPE), per_b),
                  pl.BlockSpec((1, tk, KV_LORA), tile),
                  pl.BlockSpec((1, tk, QK_ROPE), tile),
                  pl.BlockSpec((lq, KV_LORA), new),
                  pl.BlockSpec((lq, QK_ROPE), new),
                  pl.BlockSpec(wk.shape, const),
                  pl.BlockSpec(wv.shape, const),
                  pl.BlockSpec((1, QK_NOPE), const)],
        out_specs=pl.BlockSpec((lq, MLA_HEADS * V_HEAD), new),
        out_shape=jax.ShapeDtypeStruct((nb * lq, MLA_HEADS * V_HEAD), BF16),
        scratch_shapes=[pltpu.VMEM((nrow, LANES), F32), pltpu.VMEM((nrow, LANES), F32),
                        pltpu.VMEM((nrow, KV_LORA), F32)],
        compiler_params=_params(("arbitrary", "arbitrary"), 40),
        name="attn_cached",
    )(qn_bd, qr, lat_past, kr_past, lat_new, kr_new, wk, wv, g_k_nope)


def _gdn_kernel(qkv_ref, z_ref, abc_ref, abr_ref, tail0_ref, s0_ref, wc_ref, alog_c_ref, dt_c_ref, alog_r_ref,
                dt_r_ref, gout_ref, o_ref, sfin_ref, ext_ref, s_ref, *, blk, nsub, valid, n_levels):
    t = pl.program_id(1)
    nt = pl.num_programs(1)
    halo = SUBLANES
    rows = nsub * blk
    n_real = rows if valid == blk else valid

    @pl.when(t == 0)
    def _():
        ext_ref[0:halo, :] = tail0_ref[0]
        s_ref[...] = s0_ref[0]

    @pl.when(t > 0)
    def _():
        ext_ref[0:halo, :] = ext_ref[rows:rows + halo, :]

    pad_rows = lambda x: x if n_real == rows else jnp.concatenate(
        [x, jnp.zeros((rows - n_real, x.shape[1]), x.dtype)], axis=0)
    ext_ref[halo:halo + rows, :] = pad_rows(qkv_ref[0])
    wc = wc_ref[...]
    conv = wc[GDN_CONV - 1:GDN_CONV, :] * ext_ref[halo:halo + rows, :]
    for i in range(1, GDN_CONV):
        conv = conv + wc[GDN_CONV - 1 - i:GDN_CONV - i, :] * ext_ref[halo - i:halo - i + rows, :]
    act = conv * _sigmoid(conv)

    abc = pad_rows(abc_ref[0])
    abr = abr_ref[0, 0]
    rvalid = lax.broadcasted_iota(jnp.int32, (rows, 1), 0) < n_real
    cvalid = lax.broadcasted_iota(jnp.int32, (1, rows), 1) < n_real
    g_col = jnp.where(rvalid, -jnp.exp(alog_c_ref[...]) * _softplus(abc + dt_c_ref[...]), 0.0)
    beta_col = jnp.where(rvalid, _sigmoid(abc), 0.0)
    g_row = jnp.where(cvalid, -jnp.exp(alog_r_ref[...]) * _softplus(abr + dt_r_ref[...]), 0.0)
    ii = lax.broadcasted_iota(jnp.int32, (blk, blk), 0)
    jj = lax.broadcasted_iota(jnp.int32, (blk, blk), 1)
    incl = ii >= jj
    lower = incl.astype(F32)
    upper = (ii <= jj).astype(F32)
    gc_cols = [jnp.dot(lower, g_col[sb * blk:(sb + 1) * blk], preferred_element_type=F32,
                       precision=lax.Precision.HIGHEST) for sb in range(nsub)]
    gc_rows = [jnp.dot(g_row[:, sb * blk:(sb + 1) * blk], upper, preferred_element_type=F32,
                       precision=lax.Precision.HIGHEST) for sb in range(nsub)]
    eye = (ii == jj).astype(F32)
    merge_masks = []
    for lvl in range(n_levels):
        half = 1 << lvl
        merge_masks.append((ii // (2 * half) == jj // (2 * half)) & ((ii // half) % 2 == 1)
                           & ((jj // half) % 2 == 0))
    nk = GDN_HEADS * GDN_DK
    gout = gout_ref[...]

    heads = range(GDN_HEADS)
    units = [(sb, h) for sb in range(nsub) for h in heads]
    s_cur = [s_ref[h] for h in heads]
    n_out = blk if n_real == rows else n_real
    zs = {(sb, h): z_ref[0, sb * blk:sb * blk + n_out, h * GDN_DV:(h + 1) * GDN_DV] for sb, h in units}
    q, k, v, gc, beta, decay = {}, {}, {}, {}, {}, {}
    for sb, h in units:
        rs = slice(sb * blk, (sb + 1) * blk)
        qh = act[rs, h * GDN_DK:(h + 1) * GDN_DK]
        kh = act[rs, nk + h * GDN_DK: nk + (h + 1) * GDN_DK]
        u = (sb, h)
        q[u] = qh * lax.rsqrt(jnp.sum(qh * qh, axis=-1, keepdims=True) + EPS) * (float(GDN_DK) ** -0.5)
        k[u] = kh * lax.rsqrt(jnp.sum(kh * kh, axis=-1, keepdims=True) + EPS)
        v[u] = act[rs, 2 * nk + h * GDN_DV: 2 * nk + (h + 1) * GDN_DV]
        gc[u] = gc_cols[sb][:, h:h + 1]
        beta[u] = beta_col[rs, GDN_HEADS + h:GDN_HEADS + h + 1]
        decay[u] = jnp.where(incl, jnp.exp(gc[u] - gc_rows[sb][h:h + 1, :]), 0.0)
    kb = {u: k[u].astype(BF16) for u in units}
    kq = {u: lax.dot_general(jnp.concatenate([kb[u], q[u].astype(BF16)], axis=0), kb[u], NT_DIMS,
                             preferred_element_type=F32) for u in units}
    a = {u: beta[u] * kq[u][0:blk] * decay[u] for u in units}
    tinv = {u: eye - jnp.where(merge_masks[0], a[u], 0.0) for u in units}
    for lvl in range(1, n_levels):
        tb = {u: tinv[u].astype(BF16) for u in units}
        y = {u: jnp.dot(jnp.where(merge_masks[lvl], a[u], 0.0).astype(BF16), tb[u], preferred_element_type=F32)
             for u in units}
        tinv = {u: tinv[u] - jnp.dot(tb[u], y[u].astype(BF16), preferred_element_type=F32) for u in units}
    egc = {u: jnp.exp(gc[u]) for u in units}
    uw = {u: jnp.dot(tinv[u].astype(BF16),
                     jnp.concatenate([v[u] * beta[u], k[u] * (beta[u] * egc[u])], axis=1).astype(BF16),
                     preferred_element_type=F32) for u in units}
    o = {}
    for sb in range(nsub):
        sbf = [x.astype(BF16) for x in s_cur]
        ws = [jnp.dot(jnp.concatenate([uw[sb, h][:, GDN_DV:], q[sb, h] * egc[sb, h]], axis=0).astype(BF16),
                      sbf[h], preferred_element_type=F32) for h in heads]
        vb = [(uw[sb, h][:, 0:GDN_DV] - ws[h][0:blk]).astype(BF16) for h in heads]
        g_last = [gc[sb, h][blk - 1:blk, :] for h in heads]
        for h in heads:
            o[sb, h] = ws[h][blk:2 * blk] + jnp.dot((kq[sb, h][blk:2 * blk] * decay[sb, h]).astype(BF16), vb[h],
                                                    preferred_element_type=F32)
        s_cur = [s_cur[h] * jnp.exp(g_last[h])
                 + lax.dot_general((k[sb, h] * jnp.exp(g_last[h] - gc[sb, h])).astype(BF16), vb[h], TN_DIMS,
                                   preferred_element_type=F32) for h in heads]
    for h in heads:
        s_ref[h] = s_cur[h]
    for sb, h in units:
        o_ref[0, sb * blk:sb * blk + n_out, h * GDN_DV:(h + 1) * GDN_DV] = (
            _rms(o[sb, h][0:n_out], gout) * (zs[sb, h] * _sigmoid(zs[sb, h]))).astype(BF16)

    @pl.when(t == nt - 1)
    def _():
        sfin_ref[0] = s_ref[...]


def _gdn(proj3, tail0, s0, wc, a_log, dt_bias, g_out):
    nb, t, _ = proj3.shape
    blk = GDN_BLOCK
    valid = min(t, blk)
    nsub = GDN_BLOCKS_PER_STEP if t % (GDN_BLOCKS_PER_STEP * blk) == 0 else 1
    rin = nsub * valid
    rows = nsub * blk
    ab = proj3[:, :, COL_AB:COL_AB + 2 * GDN_HEADS].reshape(nb, t // rin, rin, 2 * GDN_HEADS)
    ab_rows = jnp.pad(ab.transpose(0, 1, 3, 2), ((0, 0), (0, 0), (0, 0), (0, rows - rin)))
    pad_c = lambda v: jnp.zeros((1, LANES), F32).at[0, :GDN_HEADS].set(v)
    pad_r = lambda v: jnp.zeros((2 * GDN_HEADS, 1), F32).at[:GDN_HEADS, 0].set(v)
    const2 = lambda b, i: (0, 0)
    return pl.pallas_call(
        functools.partial(_gdn_kernel, blk=blk, nsub=nsub, valid=valid, n_levels=int(math.log2(blk))),
        grid=(nb, t // rin),
        in_specs=[pl.BlockSpec((1, rin, GDN_CONV_DIM), lambda b, i: (b, i, COL_QKV // GDN_CONV_DIM)),
                  pl.BlockSpec((1, rin, GDN_OUT), lambda b, i: (b, i, COL_Z // GDN_OUT)),
                  pl.BlockSpec((1, rin, LANES), lambda b, i: (b, i, COL_AB // LANES)),
                  pl.BlockSpec((1, 1, 2 * GDN_HEADS, rows), lambda b, i: (b, i, 0, 0)),
                  pl.BlockSpec((1, SUBLANES, GDN_CONV_DIM), lambda b, i: (b, 0, 0)),
                  pl.BlockSpec((1, GDN_HEADS, GDN_DK, GDN_DV), lambda b, i: (b, 0, 0, 0)),
                  pl.BlockSpec((GDN_CONV, GDN_CONV_DIM), const2),
                  pl.BlockSpec((1, LANES), const2),
                  pl.BlockSpec((1, LANES), const2),
                  pl.BlockSpec((2 * GDN_HEADS, 1), const2),
                  pl.BlockSpec((2 * GDN_HEADS, 1), const2),
                  pl.BlockSpec((1, GDN_DV), const2)],
        out_specs=[pl.BlockSpec((1, rin, GDN_OUT), lambda b, i: (b, i, 0)),
                   pl.BlockSpec((1, GDN_HEADS, GDN_DK, GDN_DV), lambda b, i: (b, 0, 0, 0))],
        out_shape=[jax.ShapeDtypeStruct((nb, t, GDN_OUT), BF16),
                   jax.ShapeDtypeStruct((nb, GDN_HEADS, GDN_DK, GDN_DV), F32)],
        scratch_shapes=[pltpu.VMEM((rows + 2 * SUBLANES, GDN_CONV_DIM), F32),
                        pltpu.VMEM((GDN_HEADS, GDN_DK, GDN_DV), F32)],
        compiler_params=_params(("arbitrary", "arbitrary"), 40),
        name="gdn",
    )(proj3, proj3, proj3, ab_rows, tail0, s0, wc, pad_c(a_log), pad_c(dt_bias), pad_r(a_log), pad_r(dt_bias),
      g_out)


def _out_proj_kernel(oa_ref, ob_ref, w_ref, x_ref, y_ref):
    y_ref[...] = (x_ref[...]
                  + jnp.dot(oa_ref[...], w_ref[0:MLA_OUT, :], preferred_element_type=F32)
                  + jnp.dot(ob_ref[...], w_ref[MLA_OUT:MLA_OUT + GDN_OUT, :], preferred_element_type=F32))


def _out_proj(o_a, o_b, w, x, tm):
    m = x.shape[0]
    row = lambda i: (i, 0)
    return pl.pallas_call(
        _out_proj_kernel,
        grid=(m // tm,),
        in_specs=[pl.BlockSpec((tm, MLA_OUT), row),
                  pl.BlockSpec((tm, GDN_OUT), row),
                  pl.BlockSpec(w.shape, lambda i: (0, 0)),
                  pl.BlockSpec((tm, D_MODEL), row)],
        out_specs=pl.BlockSpec((tm, D_MODEL), row),
        out_shape=jax.ShapeDtypeStruct((m, D_MODEL), F32),
        compiler_params=_params(("arbitrary",), 48),
        name="out_proj",
    )(o_a, o_b, w, x)


def _ffn_kernel(x_ref, g_ref, halo0_ref, wg_ref, wu_ref, wc_ref, bc_ref, wd_ref, y_ref, st_ref, h_ref, ext_ref,
                carry_ref, *, tm, rows, halo, shift, per_seq):
    i = pl.program_id(0)
    f = pl.program_id(1)

    @pl.when(f == 0)
    def _():
        x = x_ref[...]
        h_ref[...] = _rms(x, g_ref[...]).astype(BF16)
        y_ref[...] = x

    @pl.when(i % per_seq == 0)
    def _():
        ext_ref[0:halo, :] = halo0_ref[0]

    @pl.when(i % per_seq != 0)
    def _():
        ext_ref[0:halo, :] = carry_ref[f]

    wc = wc_ref[...]
    bc = bc_ref[...]
    for r0 in range(0, tm, rows):
        h = h_ref[r0:r0 + rows, :]
        gate = jnp.dot(h, wg_ref[...], preferred_element_type=F32)
        up = jnp.dot(h, wu_ref[...], preferred_element_type=F32)
        ext_ref[halo + r0:halo + r0 + rows, :] = gate
        gc = (wc[2:3, :] * gate + wc[1:2, :] * ext_ref[halo + r0 - shift:halo + r0 - shift + rows, :]
              + wc[0:1, :] * ext_ref[halo + r0 - 2 * shift:halo + r0 - 2 * shift + rows, :] + bc)
        act = (gc * _sigmoid(gc)) * up
        y_ref[r0:r0 + rows, :] += jnp.dot(act.astype(BF16), wd_ref[...], preferred_element_type=F32)
    last = ext_ref[tm:tm + halo, :]
    carry_ref[f] = last
    st_ref[0] = last


def _ffn(x, g, halo0, wg, wu, wc, bc, wd, tm, tf, halo, shift, per_seq):
    m = x.shape[0]
    nf = D_FF // tf
    return pl.pallas_call(
        functools.partial(_ffn_kernel, tm=tm, rows=min(tm, 512), halo=halo, shift=shift, per_seq=per_seq),
        grid=(m // tm, nf),
        in_specs=[pl.BlockSpec((tm, D_MODEL), lambda i, f: (i, 0), pipeline_mode=pl.Buffered(1)),
                  pl.BlockSpec((1, D_MODEL), lambda i, f: (0, 0)),
                  pl.BlockSpec((1, halo, tf), lambda i, f: (i // per_seq, 0, f)),
                  pl.BlockSpec((D_MODEL, tf), lambda i, f: (0, f)),
                  pl.BlockSpec((D_MODEL, tf), lambda i, f: (0, f)),
                  pl.BlockSpec((FFN_CONV, tf), lambda i, f: (0, f)),
                  pl.BlockSpec((1, tf), lambda i, f: (0, f)),
                  pl.BlockSpec((tf, D_MODEL), lambda i, f: (f, 0))],
        out_specs=[pl.BlockSpec((tm, D_MODEL), lambda i, f: (i, 0)),
                   pl.BlockSpec((1, halo, tf), lambda i, f: (i, 0, f))],
        out_shape=[jax.ShapeDtypeStruct((m, D_MODEL), F32),
                   jax.ShapeDtypeStruct((m // tm, halo, D_FF), F32)],
        scratch_shapes=[pltpu.VMEM((tm, D_MODEL), BF16),
                        pltpu.VMEM((halo + tm, tf), F32),
                        pltpu.VMEM((nf, halo, tf), F32)],
        compiler_params=_params(("arbitrary", "arbitrary"), 60),
        name="conv_ffn",
    )(x, g, halo0, wg, wu, wc, bc, wd)


def _rope_tables(pos, reps):
    half = QK_ROPE // 2
    inv = 1.0 / (ROPE_THETA ** (jnp.arange(half, dtype=F32) / half))
    ang = pos.astype(F32)[:, None] * inv[None, :]
    cos, sin = jnp.cos(ang), jnp.sin(ang)
    cos = jnp.tile(jnp.concatenate([cos, cos], axis=-1), (reps, LANES // QK_ROPE))
    sin = jnp.tile(jnp.concatenate([-sin, sin], axis=-1), (reps, LANES // QK_ROPE))
    return cos, sin


def _prep_weights(lw):
    w_in = lw['w_in']
    off = np.cumsum([Q_LORA, KV_LORA, QK_ROPE, GDN_CONV_DIM, GDN_OUT, GDN_HEADS, GDN_HEADS]).tolist()
    zc = lambda n: jnp.zeros((D_MODEL, n), w_in.dtype)
    w_in_r = jnp.concatenate([w_in[:, off[2]:off[4]], w_in[:, :off[1]], w_in[:, off[1]:off[2]],
                              zc(LANES - QK_ROPE), w_in[:, off[4]:off[6]], zc(LANES - 2 * GDN_HEADS)], axis=1)
    wq = lw['w_q_up'].reshape(Q_LORA, MLA_HEADS, QK_DIM)
    wq_r = jnp.concatenate([wq[:, :, :QK_NOPE].reshape(Q_LORA, -1), wq[:, :, QK_NOPE:].reshape(Q_LORA, -1)], axis=1)
    wkv = lw['w_kv_up']
    row = lambda v: v.reshape(1, -1).astype(F32)
    return dict(
        w_in=w_in_r.astype(BF16), wq=wq_r.astype(BF16),
        wk=wkv[:, :, :QK_NOPE].reshape(KV_LORA, -1).astype(BF16),
        wv=wkv[:, :, QK_NOPE:].reshape(KV_LORA, -1).astype(BF16),
        w_out=lw['w_out'].astype(BF16), wg=lw['w_ffn_gate'].astype(BF16), wu=lw['w_ffn_up'].astype(BF16),
        wd=lw['w_ffn_down'].astype(BF16),
        g_attn=row(lw['g_attn_norm']), g_q_lat=row(lw['g_q_lat']), g_kv_lat=row(lw['g_kv_lat']),
        g_q_nope=row(lw['g_q_nope']), g_k_nope=row(lw['g_k_nope']),
        g_q_rope2=row(jnp.tile(lw['g_q_rope'], LANES // QK_ROPE)),
        g_k_rope2=row(jnp.tile(lw['g_k_rope'], LANES // QK_ROPE)),
        wc_gdn=lw['w_gdn_conv'].astype(F32), a_log=lw['a_log'].astype(F32), dt_bias=lw['dt_bias'].astype(F32),
        g_gdn_out=row(lw['g_gdn_out']), g_ffn=row(lw['g_ffn_norm']), wc_ffn=lw['w_ffn_conv'].astype(F32),
        bc_ffn=row(lw['b_ffn_conv']))


def _pad_rows_front(a, rows):
    return jnp.pad(a, ((0, 0), (rows - a.shape[1], 0), (0, 0)))


def _prompt_layer(x, w):
    nb, lb, _ = x.shape
    m = nb * lb
    xf = x.reshape(m, D_MODEL)
    proj = _norm_matmul(xf, w['g_attn'], w['w_in'], 1024, 1792)
    cos, sin = _rope_tables(jnp.arange(lb), 1)
    c_kv, k_rope, qf = _mla_pre(proj, cos, sin, w['g_q_lat'], w['g_kv_lat'], w['wq'], w['g_q_nope'],
                                w['g_q_rope2'], w['g_k_rope2'], nb, lb, 512)
    kf, v = _kv_up(c_kv.reshape(nb, lb, KV_LORA), k_rope.reshape(nb, lb, QK_ROPE), w['wk'], w['wv'],
                   w['g_k_nope'], 512)
    o_a = _flash_attention(qf, kf, v, 2048, 1024, 1024).reshape(m, MLA_OUT)

    proj3 = proj.reshape(nb, lb, D_IN_PAD)
    tail0 = jnp.zeros((nb, SUBLANES, GDN_CONV_DIM), F32)
    s0 = jnp.zeros((nb, GDN_HEADS, GDN_DK, GDN_DV), F32)
    o_b, s_new = _gdn(proj3, tail0, s0, w['wc_gdn'], w['a_log'], w['dt_bias'], w['g_gdn_out'])
    x1 = _out_proj(o_a, o_b.reshape(m, GDN_OUT), w['w_out'], xf, 512)

    tm = 1024
    halo0 = jnp.zeros((nb, SUBLANES, D_FF), F32)
    y, gate_tail = _ffn(x1, w['g_ffn'], halo0, w['wg'], w['wu'], w['wc_ffn'], w['bc_ffn'], w['wd'],
                        tm, 512, SUBLANES, 1, lb // tm)
    state = (c_kv.reshape(nb, lb, KV_LORA), k_rope.reshape(nb, lb, QK_ROPE),
             proj3[:, lb - (GDN_CONV - 1):, COL_QKV:COL_QKV + GDN_CONV_DIM], s_new,
             gate_tail.reshape(nb, lb // tm, SUBLANES, D_FF)[:, -1, SUBLANES - (FFN_CONV - 1):, :])
    return y.reshape(nb, lb, D_MODEL), state


def _sample_layer(x, lat_past, krope_past, conv_past, s_past, ffn_past, w):
    nb, lb, _ = x.shape
    past = lat_past.shape[1]
    assert (past + lb - 1) // CHUNK == past // CHUNK and past % CHUNK == 0, "new frames must share one chunk"
    m = nb * lb
    xf = x.reshape(m, D_MODEL)
    proj = _norm_matmul(xf, w['g_attn'], w['w_in'], m, 768)
    cos, sin = _rope_tables(past + jnp.arange(lb), nb)
    c_kv, k_rope, qf = _mla_pre(proj, cos, sin, w['g_q_lat'], w['g_kv_lat'], w['wq'], w['g_q_nope'],
                                w['g_q_rope2'], w['g_k_rope2'], 1, m, m)
    qh = qf[0].reshape(MLA_HEADS, nb, lb, QK_DIM).transpose(1, 0, 2, 3)
    qn_bd = jnp.einsum('bhqd,hg->bhqgd', qh[..., :QK_NOPE], jnp.eye(MLA_HEADS, dtype=qh.dtype))
    qn_bd = qn_bd.reshape(nb, MLA_HEADS * lb, MLA_HEADS * QK_NOPE)
    qr = qh[..., QK_NOPE:].reshape(nb, MLA_HEADS * lb, QK_ROPE)
    o_a = _attn_cached(qn_bd, qr, lat_past, krope_past, c_kv, k_rope, w['wk'], w['wv'], w['g_k_nope'], lb, 1024)

    proj3 = proj.reshape(nb, lb, D_IN_PAD)
    tail0 = _pad_rows_front(conv_past.astype(F32), SUBLANES)
    o_b, s_new = _gdn(proj3, tail0, s_past.astype(F32), w['wc_gdn'], w['a_log'], w['dt_bias'], w['g_gdn_out'])
    x1 = _out_proj(o_a, o_b.reshape(m, GDN_OUT), w['w_out'], xf, m)

    x1t = x1.reshape(nb, lb, D_MODEL).transpose(1, 0, 2).reshape(m, D_MODEL)
    n_hist = FFN_CONV - 1
    halo0 = ffn_past.astype(F32).transpose(1, 0, 2).reshape(1, n_hist * nb, D_FF)
    yt, gate_tail = _ffn(x1t, w['g_ffn'], halo0, w['wg'], w['wu'], w['wc_ffn'], w['bc_ffn'], w['wd'],
                         m, 512, n_hist * nb, nb, 1)
    y = yt.reshape(lb, nb, D_MODEL).transpose(1, 0, 2)
    state = (c_kv.reshape(nb, lb, KV_LORA), k_rope.reshape(nb, lb, QK_ROPE),
             proj3[:, lb - (GDN_CONV - 1):, COL_QKV:COL_QKV + GDN_CONV_DIM], s_new,
             gate_tail.reshape(n_hist, nb, D_FF).transpose(1, 0, 2))
    return y, state


def kernel(x_prompt, x_sample, cache_mla_latent, cache_mla_krope, state_gdn_conv, state_gdn_S, state_ffn_conv,
           g_attn_norm, w_in, g_q_lat, g_kv_lat, w_q_up, w_kv_up, g_q_nope, g_q_rope, g_k_nope, g_k_rope,
           w_gdn_conv, a_log, dt_bias, g_gdn_out, w_out, g_ffn_norm, w_ffn_gate, w_ffn_up, w_ffn_conv,
           b_ffn_conv, w_ffn_down):
    xp, xs = x_prompt, x_sample
    new_p, new_s = [], []
    for l in range(w_in.shape[0]):
        w = _prep_weights(dict(
            g_attn_norm=g_attn_norm[l], w_in=w_in[l], g_q_lat=g_q_lat[l], g_kv_lat=g_kv_lat[l], w_q_up=w_q_up[l],
            w_kv_up=w_kv_up[l], g_q_nope=g_q_nope[l], g_q_rope=g_q_rope[l], g_k_nope=g_k_nope[l],
            g_k_rope=g_k_rope[l], w_gdn_conv=w_gdn_conv[l], a_log=a_log[l], dt_bias=dt_bias[l],
            g_gdn_out=g_gdn_out[l], w_out=w_out[l], g_ffn_norm=g_ffn_norm[l], w_ffn_gate=w_ffn_gate[l],
            w_ffn_up=w_ffn_up[l], w_ffn_conv=w_ffn_conv[l], b_ffn_conv=b_ffn_conv[l], w_ffn_down=w_ffn_down[l]))
        xp, st_p = _prompt_layer(xp, w)
        xs, st_s = _sample_layer(xs, cache_mla_latent[l], cache_mla_krope[l], state_gdn_conv[l], state_gdn_S[l],
                                 state_ffn_conv[l], w)
        new_p.append(st_p)
        new_s.append(st_s)
    p_state = [jnp.stack(t) for t in zip(*new_p)]
    s_state = [jnp.stack(t) for t in zip(*new_s)]
    return (xp, xs, *p_state, *s_state)
```

```python
import functools
import math

import jax
import jax.numpy as jnp
import numpy as np
from jax import lax
from jax.experimental import pallas as pl
from jax.experimental.pallas import tpu as pltpu

D_MODEL = 2048
CHUNK = 64
EPS = 1e-6
MLA_HEADS = 8
Q_LORA = 512
KV_LORA = 512
QK_NOPE = 128
QK_ROPE = 64
V_HEAD = 128
ROPE_THETA = 10000.0
GDN_HEADS = 8
GDN_DK = 128
GDN_DV = 128
GDN_CONV = 4
GDN_CONV_DIM = 2 * GDN_HEADS * GDN_DK + GDN_HEADS * GDN_DV
D_FF = 5632
FFN_CONV = 3
MLA_OUT = MLA_HEADS * V_HEAD
GDN_OUT = GDN_HEADS * GDN_DV
QK_DIM = QK_NOPE + QK_ROPE

LANES = 128
SUBLANES = 8
GDN_BLOCK = 128
GDN_BLOCKS_PER_STEP = 2

COL_QKV = 0
COL_Z = COL_QKV + GDN_CONV_DIM
COL_QA = COL_Z + GDN_OUT
COL_KVA = COL_QA + Q_LORA
COL_KR = COL_KVA + KV_LORA
COL_AB = COL_KR + LANES
D_IN_PAD = COL_AB + LANES

BF16 = jnp.bfloat16
F32 = jnp.float32
NT_DIMS = (((1,), (1,)), ((), ()))
TN_DIMS = (((0,), (0,)), ((), ()))


def _params(sem, vmem_mb):
    return pltpu.CompilerParams(dimension_semantics=sem, vmem_limit_bytes=vmem_mb * 1024 * 1024)


def _rms(x, g):
    return x * lax.rsqrt(jnp.mean(x * x, axis=-1, keepdims=True) + EPS) * g


def _sigmoid(x):
    return 1.0 / (1.0 + jnp.exp(-x))


def _softplus(x):
    return jnp.maximum(x, 0.0) + jnp.log(1.0 + jnp.exp(-jnp.abs(x)))


def _norm_matmul_kernel(x_ref, g_ref, w_ref, o_ref, h_ref):
    @pl.when(pl.program_id(1) == 0)
    def _():
        h_ref[...] = _rms(x_ref[...], g_ref[...]).astype(BF16)

    o_ref[...] = jnp.dot(h_ref[...], w_ref[...], preferred_element_type=F32)


def _norm_matmul(x, g, w, tm, tn):
    m, k = x.shape
    n = w.shape[1]
    return pl.pallas_call(
        _norm_matmul_kernel,
        grid=(m // tm, n // tn),
        in_specs=[pl.BlockSpec((tm, k), lambda i, j: (i, 0)),
                  pl.BlockSpec((1, k), lambda i, j: (0, 0)),
                  pl.BlockSpec((k, tn), lambda i, j: (0, j))],
        out_specs=pl.BlockSpec((tm, tn), lambda i, j: (i, j)),
        out_shape=jax.ShapeDtypeStruct((m, n), F32),
        scratch_shapes=[pltpu.VMEM((tm, k), BF16)],
        compiler_params=_params(("arbitrary", "arbitrary"), 56),
        name="in_proj",
    )(x, g, w)


def _rope_pairs(y, cos, sin):
    lane = lax.broadcasted_iota(jnp.int32, (1, LANES), 1)
    first_half = (lane % QK_ROPE) < (QK_ROPE // 2)
    swapped = jnp.where(first_half, pltpu.roll(y, LANES - QK_ROPE // 2, 1), pltpu.roll(y, QK_ROPE // 2, 1))
    return y * cos + swapped * sin


def _mla_pre_kernel(qa_ref, kva_ref, kr_ref, cos_ref, sin_ref, gq_ref, gkv_ref, wq_ref, gqn_ref, gqr_ref,
                    gkr_ref, ckv_ref, krope_ref, qf_ref, *, scale):
    cos = cos_ref[...]
    sin = sin_ref[...]
    lane = lax.broadcasted_iota(jnp.int32, (1, LANES), 1)
    lo = lane < QK_ROPE

    ckv_ref[...] = _rms(kva_ref[...], gkv_ref[...])

    kr = kr_ref[...]
    ss = jnp.sum(jnp.where(lo, kr * kr, 0.0), axis=-1, keepdims=True)
    kr = kr * lax.rsqrt(ss * (1.0 / QK_ROPE) + EPS) * gkr_ref[...]
    krope_ref[...] = _rope_pairs(kr, cos, sin)[:, :QK_ROPE]

    hq = _rms(qa_ref[...], gq_ref[...]).astype(BF16)
    q = jnp.dot(hq, wq_ref[...], preferred_element_type=F32)
    gqn = gqn_ref[...] * scale
    for h in range(MLA_HEADS):
        xn = q[:, h * QK_NOPE:(h + 1) * QK_NOPE]
        qf_ref[0, h, :, 0:QK_NOPE] = (_rms(xn, gqn)).astype(BF16)
    gqr = gqr_ref[...] * scale
    rope0 = MLA_HEADS * QK_NOPE
    for p in range(MLA_HEADS // 2):
        xr = q[:, rope0 + p * LANES: rope0 + (p + 1) * LANES]
        sq = xr * xr
        s_lo = jnp.sum(jnp.where(lo, sq, 0.0), axis=-1, keepdims=True)
        s_hi = jnp.sum(jnp.where(lo, 0.0, sq), axis=-1, keepdims=True)
        r = jnp.where(lo, lax.rsqrt(s_lo * (1.0 / QK_ROPE) + EPS), lax.rsqrt(s_hi * (1.0 / QK_ROPE) + EPS))
        ro = _rope_pairs(xr * r * gqr, cos, sin).astype(BF16)
        qf_ref[0, 2 * p, :, QK_NOPE:QK_DIM] = ro[:, :QK_ROPE]
        qf_ref[0, 2 * p + 1, :, QK_NOPE:QK_DIM] = ro[:, QK_ROPE:]


def _mla_pre(proj, cos, sin, g_q_lat, g_kv_lat, wq, g_q_nope, g_q_rope2, g_k_rope2, nb, lb, tm):
    m = proj.shape[0]
    per_seq = lb // tm
    n_tab = cos.shape[0] // tm
    scale = float(QK_DIM) ** -0.5 * math.log2(math.e)
    row = lambda i: (i, 0)
    const = lambda i: (0, 0)
    return pl.pallas_call(
        functools.partial(_mla_pre_kernel, scale=scale),
        grid=(m // tm,),
        in_specs=[pl.BlockSpec((tm, Q_LORA), lambda i: (i, COL_QA // Q_LORA)),
                  pl.BlockSpec((tm, KV_LORA), lambda i: (i, COL_KVA // KV_LORA)),
                  pl.BlockSpec((tm, LANES), lambda i: (i, COL_KR // LANES)),
                  pl.BlockSpec((tm, LANES), lambda i: (i % n_tab, 0)),
                  pl.BlockSpec((tm, LANES), lambda i: (i % n_tab, 0)),
                  pl.BlockSpec((1, Q_LORA), const),
                  pl.BlockSpec((1, KV_LORA), const),
                  pl.BlockSpec(wq.shape, const),
                  pl.BlockSpec((1, QK_NOPE), const),
                  pl.BlockSpec((1, LANES), const),
                  pl.BlockSpec((1, LANES), const)],
        out_specs=[pl.BlockSpec((tm, KV_LORA), row),
                   pl.BlockSpec((tm, QK_ROPE), row),
                   pl.BlockSpec((1, MLA_HEADS, tm, QK_DIM), lambda i: (i // per_seq, 0, i % per_seq, 0))],
        out_shape=[jax.ShapeDtypeStruct((m, KV_LORA), F32),
                   jax.ShapeDtypeStruct((m, QK_ROPE), F32),
                   jax.ShapeDtypeStruct((nb, MLA_HEADS, lb, QK_DIM), BF16)],
        compiler_params=_params(("arbitrary",), 40),
        name="mla_pre",
    )(proj, proj, proj, cos, sin, g_q_lat, g_kv_lat, wq, g_q_nope, g_q_rope2, g_k_rope2)


def _kv_up_kernel(lat_ref, kr_ref, wk_ref, wv_ref, gk_ref, kf_ref, v_ref):
    lat = lat_ref[0].astype(BF16)
    kn = jnp.dot(lat, wk_ref[...], preferred_element_type=F32)
    vv = jnp.dot(lat, wv_ref[...], preferred_element_type=F32)
    kr = kr_ref[0].astype(BF16)
    gk = gk_ref[...]
    for h in range(MLA_HEADS):
        kf_ref[0, h, :, 0:QK_NOPE] = _rms(kn[:, h * QK_NOPE:(h + 1) * QK_NOPE], gk).astype(BF16)
        kf_ref[0, h, :, QK_NOPE:QK_DIM] = kr
        v_ref[0, h] = vv[:, h * V_HEAD:(h + 1) * V_HEAD].astype(BF16)


def _kv_up(lat, krope, wk, wv, g_k_nope, tm):
    nb, t, _ = lat.shape
    const = lambda b, i: (0, 0)
    return pl.pallas_call(
        _kv_up_kernel,
        grid=(nb, t // tm),
        in_specs=[pl.BlockSpec((1, tm, KV_LORA), lambda b, i: (b, i, 0)),
                  pl.BlockSpec((1, tm, QK_ROPE), lambda b, i: (b, i, 0)),
                  pl.BlockSpec(wk.shape, const),
                  pl.BlockSpec(wv.shape, const),
                  pl.BlockSpec((1, QK_NOPE), const)],
        out_specs=[pl.BlockSpec((1, MLA_HEADS, tm, QK_DIM), lambda b, i: (b, 0, i, 0)),
                   pl.BlockSpec((1, MLA_HEADS, tm, V_HEAD), lambda b, i: (b, 0, i, 0))],
        out_shape=[jax.ShapeDtypeStruct((nb, MLA_HEADS, t, QK_DIM), BF16),
                   jax.ShapeDtypeStruct((nb, MLA_HEADS, t, V_HEAD), BF16)],
        compiler_params=_params(("arbitrary", "arbitrary"), 40),
        name="kv_up",
    )(lat, krope, wk, wv, g_k_nope)


def _flash_kernel(q_ref, k_ref, v_ref, o_ref, m_ref, l_ref, acc_ref, *, tq, tk, sub):
    qi = pl.program_id(2)
    m_ref[...] = jnp.full(m_ref.shape, -jnp.inf, F32)
    l_ref[...] = jnp.zeros(l_ref.shape, F32)
    acc_ref[...] = jnp.zeros(acc_ref.shape, F32)

    def attend_rows(r0, nr, keys, k0, masked):
        rows = pl.ds(r0, nr)
        s = lax.dot_general(q_ref[0, 0, rows, :], k_ref[0, 0, keys, :], NT_DIMS, preferred_element_type=F32)
        if masked:
            rc = (r0 + lax.broadcasted_iota(jnp.int32, (nr, tk), 0)) // CHUNK
            cc = (k0 + lax.broadcasted_iota(jnp.int32, (nr, tk), 1)) // CHUNK
            s = jnp.where(cc <= rc, s, -jnp.inf)
        m = m_ref[rows, :]
        m_new = jnp.maximum(m, jnp.max(s, axis=-1, keepdims=True))
        alpha = jnp.exp2(m - m_new)
        pc = [jnp.exp2(s[:, c * LANES:(c + 1) * LANES] - m_new) for c in range(tk // LANES)]
        psum = pc[0]
        for c in range(1, tk // LANES):
            psum = psum + pc[c]
        p = jnp.concatenate(pc, axis=1)
        m_ref[rows, :] = m_new
        l_ref[rows, :] = alpha * l_ref[rows, :] + psum
        acc_ref[rows, :] = alpha * acc_ref[rows, :] + jnp.dot(p.astype(BF16), v_ref[0, 0, keys, :],
                                                              preferred_element_type=F32)

    def attend(r_lo, r_hi, ki, k0, masked):
        keys = pl.ds(pl.multiple_of(ki * tk, tk), tk)
        for r0 in range(r_lo, r_hi, sub):
            if not masked or r0 + sub > k0:
                attend_rows(r0, sub, keys, k0, masked and r0 < k0 + tk)

    def body(ki, c):
        attend(0, tq, ki, 0, False)
        return c

    n_diag = tq // tk
    lax.fori_loop(0, n_diag * qi, body, 0)
    for j in range(n_diag):
        attend(0, tq, n_diag * qi + j, j * tk, True)
    l = jnp.sum(l_ref[...], axis=-1, keepdims=True)
    o_ref[0] = (acc_ref[...] / l).astype(BF16)


def _flash_attention(qf, kf, v, tq, tk, sub):
    nb, nh, lq, _ = qf.shape
    t = kf.shape[2]
    return pl.pallas_call(
        functools.partial(_flash_kernel, tq=tq, tk=tk, sub=sub),
        grid=(nb, nh, lq // tq),
        in_specs=[pl.BlockSpec((1, 1, tq, QK_DIM), lambda b, h, i: (b, h, i, 0)),
                  pl.BlockSpec((1, 1, t, QK_DIM), lambda b, h, i: (b, h, 0, 0)),
                  pl.BlockSpec((1, 1, t, V_HEAD), lambda b, h, i: (b, h, 0, 0))],
        out_specs=pl.BlockSpec((1, tq, V_HEAD), lambda b, h, i: (b, i, h)),
        out_shape=jax.ShapeDtypeStruct((nb, lq, nh * V_HEAD), BF16),
        scratch_shapes=[pltpu.VMEM((tq, LANES), F32), pltpu.VMEM((tq, LANES), F32),
                        pltpu.VMEM((tq, V_HEAD), F32)],
        compiler_params=_params(("arbitrary", "arbitrary", "arbitrary"), 40),
        name="flash_attn",
    )(qf, kf, v)


def _attn_cached_kernel(qn_ref, qr_ref, lat_ref, kr_ref, latn_ref, krn_ref, wk_ref, wv_ref, gk_ref, o_ref,
                        m_ref, l_ref, acc_ref, *, lq):
    kt = pl.program_id(1)

    @pl.when(kt == 0)
    def _():
        m_ref[...] = jnp.full(m_ref.shape, -jnp.inf, F32)
        l_ref[...] = jnp.zeros(l_ref.shape, F32)
        acc_ref[...] = jnp.zeros(acc_ref.shape, F32)

    def attend(lat, kr):
        latb = lat.astype(BF16)
        kn = jnp.dot(latb, wk_ref[...], preferred_element_type=F32)
        gk = gk_ref[...]
        knb = jnp.concatenate([_rms(kn[:, h * QK_NOPE:(h + 1) * QK_NOPE], gk).astype(BF16)
                               for h in range(MLA_HEADS)], axis=1)
        s = (lax.dot_general(qn_ref[0], knb, NT_DIMS, preferred_element_type=F32)
             + lax.dot_general(qr_ref[0], kr.astype(BF16), NT_DIMS, preferred_element_type=F32))
        m = m_ref[...]
        m_new = jnp.maximum(m, jnp.max(s, axis=-1, keepdims=True))
        alpha = jnp.exp2(m - m_new)
        p = jnp.exp2(s - m_new[:, 0:1])
        m_ref[...] = m_new
        l_ref[...] = alpha * l_ref[...] + jnp.sum(p, axis=-1, keepdims=True)
        acc_ref[...] = alpha[:, 0:1] * acc_ref[...] + jnp.dot(p.astype(BF16), latb, preferred_element_type=F32)

    attend(lat_ref[0], kr_ref[0])

    @pl.when(kt == pl.num_programs(1) - 1)
    def _():
        attend(latn_ref[...], krn_ref[...])
        o_lat = (acc_ref[...] / l_ref[:, 0:1]).astype(BF16)
        for h in range(MLA_HEADS):
            o_ref[:, h * V_HEAD:(h + 1) * V_HEAD] = jnp.dot(
                o_lat[h * lq:(h + 1) * lq, :], wv_ref[:, h * V_HEAD:(h + 1) * V_HEAD],
                preferred_element_type=F32).astype(BF16)


def _attn_cached(qn_bd, qr, lat_past, kr_past, lat_new, kr_new, wk, wv, g_k_nope, lq, tk):
    nb, past, _ = lat_past.shape
    nrow = MLA_HEADS * lq
    per_b = lambda b, k: (b, 0, 0)
    tile = lambda b, k: (b, k, 0)
    new = lambda b, k: (b, 0)
    const = lambda b, k: (0, 0)
    return pl.pallas_call(
        functools.partial(_attn_cached_kernel, lq=lq),
        grid=(nb, past // tk),
        in_specs=[pl.BlockSpec((1, nrow, MLA_HEADS * QK_NOPE), per_b),
                  pl.BlockSpec((1, nrow, QK_ROPE), per_b),
                  pl.BlockSpec((1, tk, KV_LORA), tile),
                  pl.BlockSpec((1, tk, QK_ROPE), tile),
                  pl.BlockSpec((lq, KV_LORA), new),
                  pl.BlockSpec((lq, QK_ROPE), new),
                  pl.BlockSpec(wk.shape, const),
                  pl.BlockSpec(wv.shape, const),
                  pl.BlockSpec((1, QK_NOPE), const)],
        out_specs=pl.BlockSpec((lq, MLA_HEADS * V_HEAD), new),
        out_shape=jax.ShapeDtypeStruct((nb * lq, MLA_HEADS * V_HEAD), BF16),
        scratch_shapes=[pltpu.VMEM((nrow, LANES), F32), pltpu.VMEM((nrow, LANES), F32),
                        pltpu.VMEM((nrow, KV_LORA), F32)],
        compiler_params=_params(("arbitrary", "arbitrary"), 40),
        name="attn_cached",
    )(qn_bd, qr, lat_past, kr_past, lat_new, kr_new, wk, wv, g_k_nope)


def _gdn_kernel(qkv_ref, z_ref, abc_ref, abr_ref, tail0_ref, s0_ref, wc_ref, alog_c_ref, dt_c_ref, alog_r_ref,
                dt_r_ref, gout_ref, o_ref, sfin_ref, ext_ref, s_ref, *, blk, nsub, valid, n_levels):
    t = pl.program_id(1)
    nt = pl.num_programs(1)
    halo = SUBLANES
    rows = nsub * blk
    n_real = rows if valid == blk else valid

    @pl.when(t == 0)
    def _():
        ext_ref[0:halo, :] = tail0_ref[0]
        s_ref[...] = s0_ref[0]

    @pl.when(t > 0)
    def _():
        ext_ref[0:halo, :] = ext_ref[rows:rows + halo, :]

    def pad_rows(x, to=rows):
        return x if x.shape[0] == to else jnp.concatenate(
            [x, jnp.zeros((to - x.shape[0], x.shape[1]), x.dtype)], axis=0)

    conv_rows = -(-n_real // SUBLANES) * SUBLANES
    ext_ref[halo:halo + conv_rows, :] = pad_rows(qkv_ref[0], conv_rows)
    wc = wc_ref[...]
    conv = wc[GDN_CONV - 1:GDN_CONV, :] * ext_ref[halo:halo + conv_rows, :]
    for i in range(1, GDN_CONV):
        conv = conv + wc[GDN_CONV - 1 - i:GDN_CONV - i, :] * ext_ref[halo - i:halo - i + conv_rows, :]
    act = pad_rows(conv * _sigmoid(conv))

    abc = pad_rows(abc_ref[0])
    abr = abr_ref[0, 0]
    rvalid = lax.broadcasted_iota(jnp.int32, (rows, 1), 0) < n_real
    cvalid = lax.broadcasted_iota(jnp.int32, (1, rows), 1) < n_real
    g_col = jnp.where(rvalid, -jnp.exp(alog_c_ref[...]) * _softplus(abc + dt_c_ref[...]), 0.0)
    beta_col = jnp.where(rvalid, _sigmoid(abc), 0.0)
    g_row = jnp.where(cvalid, -jnp.exp(alog_r_ref[...]) * _softplus(abr + dt_r_ref[...]), 0.0)
    ii = lax.broadcasted_iota(jnp.int32, (blk, blk), 0)
    jj = lax.broadcasted_iota(jnp.int32, (blk, blk), 1)
    incl = ii >= jj
    lower = incl.astype(F32)
    upper = (ii <= jj).astype(F32)
    gc_cols = [jnp.dot(lower, g_col[sb * blk:(sb + 1) * blk], preferred_element_type=F32,
                       precision=lax.Precision.HIGHEST) for sb in range(nsub)]
    gc_rows = [jnp.dot(g_row[:, sb * blk:(sb + 1) * blk], upper, preferred_element_type=F32,
                       precision=lax.Precision.HIGHEST) for sb in range(nsub)]
    eye = (ii == jj).astype(F32)
    merge_masks = []
    for lvl in range(n_levels):
        half = 1 << lvl
        merge_masks.append((ii // (2 * half) == jj // (2 * half)) & ((ii // half) % 2 == 1)
                           & ((jj // half) % 2 == 0))
    nk = GDN_HEADS * GDN_DK
    gout = gout_ref[...]

    heads = range(GDN_HEADS)
    units = [(sb, h) for sb in range(nsub) for h in heads]
    s_cur = [s_ref[h] for h in heads]
    n_out = blk if n_real == rows else n_real
    zs = {(sb, h): z_ref[0, sb * blk:sb * blk + n_out, h * GDN_DV:(h + 1) * GDN_DV] for sb, h in units}
    q, k, v, gc, beta, decay = {}, {}, {}, {}, {}, {}
    for sb, h in units:
        rs = slice(sb * blk, (sb + 1) * blk)
        qh = act[rs, h * GDN_DK:(h + 1) * GDN_DK]
        kh = act[rs, nk + h * GDN_DK: nk + (h + 1) * GDN_DK]
        u = (sb, h)
        q[u] = qh * lax.rsqrt(jnp.sum(qh * qh, axis=-1, keepdims=True) + EPS) * (float(GDN_DK) ** -0.5)
        k[u] = kh * lax.rsqrt(jnp.sum(kh * kh, axis=-1, keepdims=True) + EPS)
        v[u] = act[rs, 2 * nk + h * GDN_DV: 2 * nk + (h + 1) * GDN_DV]
        gc[u] = gc_cols[sb][:, h:h + 1]
        beta[u] = beta_col[rs, GDN_HEADS + h:GDN_HEADS + h + 1]
        decay[u] = jnp.where(incl, jnp.exp(gc[u] - gc_rows[sb][h:h + 1, :]), 0.0)
    kb = {u: k[u].astype(BF16) for u in units}
    kq = {u: lax.dot_general(jnp.concatenate([kb[u], q[u].astype(BF16)], axis=0), kb[u], NT_DIMS,
                             preferred_element_type=F32) for u in units}
    a = {u: beta[u] * kq[u][0:blk] * decay[u] for u in units}
    tinv = {u: eye - jnp.where(merge_masks[0], a[u], 0.0) for u in units}
    for lvl in range(1, n_levels):
        tb = {u: tinv[u].astype(BF16) for u in units}
        y = {u: jnp.dot(jnp.where(merge_masks[lvl], a[u], 0.0).astype(BF16), tb[u], preferred_element_type=F32)
             for u in units}
        tinv = {u: tinv[u] - jnp.dot(tb[u], y[u].astype(BF16), preferred_element_type=F32) for u in units}
    egc = {u: jnp.exp(gc[u]) for u in units}
    uw = {u: jnp.dot(tinv[u].astype(BF16),
                     jnp.concatenate([v[u] * beta[u], k[u] * (beta[u] * egc[u])], axis=1).astype(BF16),
                     preferred_element_type=F32) for u in units}
    o = {}
    for sb in range(nsub):
        sbf = [x.astype(BF16) for x in s_cur]
        ws = [jnp.dot(jnp.concatenate([uw[sb, h][:, GDN_DV:], q[sb, h] * egc[sb, h]], axis=0).astype(BF16),
                      sbf[h], preferred_element_type=F32) for h in heads]
        vb = [(uw[sb, h][:, 0:GDN_DV] - ws[h][0:blk]).astype(BF16) for h in heads]
        g_last = [gc[sb, h][blk - 1:blk, :] for h in heads]
        for h in heads:
            o[sb, h] = ws[h][blk:2 * blk] + jnp.dot((kq[sb, h][blk:2 * blk] * decay[sb, h]).astype(BF16), vb[h],
                                                    preferred_element_type=F32)
        s_cur = [s_cur[h] * jnp.exp(g_last[h])
                 + lax.dot_general((k[sb, h] * jnp.exp(g_last[h] - gc[sb, h])).astype(BF16), vb[h], TN_DIMS,
                                   preferred_element_type=F32) for h in heads]
    for h in heads:
        s_ref[h] = s_cur[h]
    for sb, h in units:
        o_ref[0, sb * blk:sb * blk + n_out, h * GDN_DV:(h + 1) * GDN_DV] = (
            _rms(o[sb, h][0:n_out], gout) * (zs[sb, h] * _sigmoid(zs[sb, h]))).astype(BF16)

    @pl.when(t == nt - 1)
    def _():
        sfin_ref[0] = s_ref[...]


def _gdn(proj3, tail0, s0, wc, a_log, dt_bias, g_out):
    nb, t, _ = proj3.shape
    blk = GDN_BLOCK
    valid = min(t, blk)
    nsub = GDN_BLOCKS_PER_STEP if t % (GDN_BLOCKS_PER_STEP * blk) == 0 else 1
    rin = nsub * valid
    rows = nsub * blk
    ab = proj3[:, :, COL_AB:COL_AB + 2 * GDN_HEADS].reshape(nb, t // rin, rin, 2 * GDN_HEADS)
    ab_rows = jnp.pad(ab.transpose(0, 1, 3, 2), ((0, 0), (0, 0), (0, 0), (0, rows - rin)))
    pad_c = lambda v: jnp.zeros((1, LANES), F32).at[0, :GDN_HEADS].set(v)
    pad_r = lambda v: jnp.zeros((2 * GDN_HEADS, 1), F32).at[:GDN_HEADS, 0].set(v)
    const2 = lambda b, i: (0, 0)
    return pl.pallas_call(
        functools.partial(_gdn_kernel, blk=blk, nsub=nsub, valid=valid, n_levels=int(math.log2(blk))),
        grid=(nb, t // rin),
        in_specs=[pl.BlockSpec((1, rin, GDN_CONV_DIM), lambda b, i: (b, i, COL_QKV // GDN_CONV_DIM)),
                  pl.BlockSpec((1, rin, GDN_OUT), lambda b, i: (b, i, COL_Z // GDN_OUT)),
                  pl.BlockSpec((1, rin, LANES), lambda b, i: (b, i, COL_AB // LANES)),
                  pl.BlockSpec((1, 1, 2 * GDN_HEADS, rows), lambda b, i: (b, i, 0, 0)),
                  pl.BlockSpec((1, SUBLANES, GDN_CONV_DIM), lambda b, i: (b, 0, 0)),
                  pl.BlockSpec((1, GDN_HEADS, GDN_DK, GDN_DV), lambda b, i: (b, 0, 0, 0)),
                  pl.BlockSpec((GDN_CONV, GDN_CONV_DIM), const2),
                  pl.BlockSpec((1, LANES), const2),
                  pl.BlockSpec((1, LANES), const2),
                  pl.BlockSpec((2 * GDN_HEADS, 1), const2),
                  pl.BlockSpec((2 * GDN_HEADS, 1), const2),
                  pl.BlockSpec((1, GDN_DV), const2)],
        out_specs=[pl.BlockSpec((1, rin, GDN_OUT), lambda b, i: (b, i, 0)),
                   pl.BlockSpec((1, GDN_HEADS, GDN_DK, GDN_DV), lambda b, i: (b, 0, 0, 0))],
        out_shape=[jax.ShapeDtypeStruct((nb, t, GDN_OUT), BF16),
                   jax.ShapeDtypeStruct((nb, GDN_HEADS, GDN_DK, GDN_DV), F32)],
        scratch_shapes=[pltpu.VMEM((rows + 2 * SUBLANES, GDN_CONV_DIM), F32),
                        pltpu.VMEM((GDN_HEADS, GDN_DK, GDN_DV), F32)],
        compiler_params=_params(("arbitrary", "arbitrary"), 40),
        name="gdn",
    )(proj3, proj3, proj3, ab_rows, tail0, s0, wc, pad_c(a_log), pad_c(dt_bias), pad_r(a_log), pad_r(dt_bias),
      g_out)


def _out_proj_kernel(oa_ref, ob_ref, w_ref, x_ref, g_ref, y_ref, h_ref):
    tm = x_ref.shape[0]
    rows = min(tm, 256)
    for r0 in range(0, tm, rows):
        rs = slice(r0, r0 + rows)
        y = (x_ref[rs, :]
             + jnp.dot(oa_ref[rs, :], w_ref[0:MLA_OUT, :], preferred_element_type=F32)
             + jnp.dot(ob_ref[rs, :], w_ref[MLA_OUT:MLA_OUT + GDN_OUT, :], preferred_element_type=F32))
        y_ref[rs, :] = y
        h_ref[rs, :] = _rms(y, g_ref[...]).astype(BF16)


def _out_proj(o_a, o_b, w, x, g_next, tm):
    m = x.shape[0]
    row = lambda i: (i, 0)
    return pl.pallas_call(
        _out_proj_kernel,
        grid=(m // tm,),
        in_specs=[pl.BlockSpec((tm, MLA_OUT), row),
                  pl.BlockSpec((tm, GDN_OUT), row),
                  pl.BlockSpec(w.shape, lambda i: (0, 0)),
                  pl.BlockSpec((tm, D_MODEL), row),
                  pl.BlockSpec((1, D_MODEL), lambda i: (0, 0))],
        out_specs=[pl.BlockSpec((tm, D_MODEL), row), pl.BlockSpec((tm, D_MODEL), row)],
        out_shape=[jax.ShapeDtypeStruct((m, D_MODEL), F32), jax.ShapeDtypeStruct((m, D_MODEL), BF16)],
        compiler_params=_params(("arbitrary",), 48),
        name="out_proj",
    )(o_a, o_b, w, x, g_next)


def _ffn_kernel(x_ref, h_ref, halo0_ref, wg_ref, wu_ref, wc_ref, bc_ref, wd_ref, y_ref, st_ref, ext_ref,
                carry_ref, *, tm, rows, halo, shift, per_seq):
    i = pl.program_id(0)
    f = pl.program_id(1)

    @pl.when(f == 0)
    def _():
        y_ref[...] = x_ref[...]

    @pl.when(i % per_seq == 0)
    def _():
        ext_ref[0:halo, :] = halo0_ref[0]

    @pl.when(i % per_seq != 0)
    def _():
        ext_ref[0:halo, :] = carry_ref[f]

    wc = wc_ref[...]
    bc = bc_ref[...]
    for r0 in range(0, tm, rows):
        h = h_ref[r0:r0 + rows, :]
        gate = jnp.dot(h, wg_ref[...], preferred_element_type=F32)
        up = jnp.dot(h, wu_ref[...], preferred_element_type=F32)
        ext_ref[halo + r0:halo + r0 + rows, :] = gate
        gc = (wc[2:3, :] * gate + wc[1:2, :] * ext_ref[halo + r0 - shift:halo + r0 - shift + rows, :]
              + wc[0:1, :] * ext_ref[halo + r0 - 2 * shift:halo + r0 - 2 * shift + rows, :] + bc)
        act = (gc * _sigmoid(gc)) * up
        y_ref[r0:r0 + rows, :] += jnp.dot(act.astype(BF16), wd_ref[...], preferred_element_type=F32)
    last = ext_ref[tm:tm + halo, :]
    carry_ref[f] = last
    st_ref[0] = last


def _ffn(x, h, halo0, wg, wu, wc, bc, wd, tm, tf, halo, shift, per_seq):
    m = x.shape[0]
    nf = D_FF // tf
    return pl.pallas_call(
        functools.partial(_ffn_kernel, tm=tm, rows=min(tm, 512), halo=halo, shift=shift, per_seq=per_seq),
        grid=(m // tm, nf),
        in_specs=[pl.BlockSpec((tm, D_MODEL), lambda i, f: (i, 0), pipeline_mode=pl.Buffered(1)),
                  pl.BlockSpec((tm, D_MODEL), lambda i, f: (i, 0)),
                  pl.BlockSpec((1, halo, tf), lambda i, f: (i // per_seq, 0, f)),
                  pl.BlockSpec((D_MODEL, tf), lambda i, f: (0, f)),
                  pl.BlockSpec((D_MODEL, tf), lambda i, f: (0, f)),
                  pl.BlockSpec((FFN_CONV, tf), lambda i, f: (0, f)),
                  pl.BlockSpec((1, tf), lambda i, f: (0, f)),
                  pl.BlockSpec((tf, D_MODEL), lambda i, f: (f, 0))],
        out_specs=[pl.BlockSpec((tm, D_MODEL), lambda i, f: (i, 0)),
                   pl.BlockSpec((1, halo, tf), lambda i, f: (i, 0, f))],
        out_shape=[jax.ShapeDtypeStruct((m, D_MODEL), F32),
                   jax.ShapeDtypeStruct((m // tm, halo, D_FF), F32)],
        scratch_shapes=[pltpu.VMEM((halo + tm, tf), F32),
                        pltpu.VMEM((nf, halo, tf), F32)],
        compiler_params=_params(("arbitrary", "arbitrary"), 60),
        name="conv_ffn",
    )(x, h, halo0, wg, wu, wc, bc, wd)


def _rope_tables(pos, reps):
    half = QK_ROPE // 2
    inv = 1.0 / (ROPE_THETA ** (jnp.arange(half, dtype=F32) / half))
    ang = pos.astype(F32)[:, None] * inv[None, :]
    cos, sin = jnp.cos(ang), jnp.sin(ang)
    cos = jnp.tile(jnp.concatenate([cos, cos], axis=-1), (reps, LANES // QK_ROPE))
    sin = jnp.tile(jnp.concatenate([-sin, sin], axis=-1), (reps, LANES // QK_ROPE))
    return cos, sin


def _prep_weights(lw):
    w_in = lw['w_in']
    off = np.cumsum([Q_LORA, KV_LORA, QK_ROPE, GDN_CONV_DIM, GDN_OUT, GDN_HEADS, GDN_HEADS]).tolist()
    zc = lambda n: jnp.zeros((D_MODEL, n), w_in.dtype)
    w_in_r = jnp.concatenate([w_in[:, off[2]:off[4]], w_in[:, :off[1]], w_in[:, off[1]:off[2]],
                              zc(LANES - QK_ROPE), w_in[:, off[4]:off[6]], zc(LANES - 2 * GDN_HEADS)], axis=1)
    wq = lw['w_q_up'].reshape(Q_LORA, MLA_HEADS, QK_DIM)
    wq_r = jnp.concatenate([wq[:, :, :QK_NOPE].reshape(Q_LORA, -1), wq[:, :, QK_NOPE:].reshape(Q_LORA, -1)], axis=1)
    wkv = lw['w_kv_up']
    row = lambda v: v.reshape(1, -1).astype(F32)
    return dict(
        w_in=w_in_r.astype(BF16), wq=wq_r.astype(BF16),
        wk=wkv[:, :, :QK_NOPE].reshape(KV_LORA, -1).astype(BF16),
        wv=wkv[:, :, QK_NOPE:].reshape(KV_LORA, -1).astype(BF16),
        w_out=lw['w_out'].astype(BF16), wg=lw['w_ffn_gate'].astype(BF16), wu=lw['w_ffn_up'].astype(BF16),
        wd=lw['w_ffn_down'].astype(BF16),
        g_attn=row(lw['g_attn_norm']), g_q_lat=row(lw['g_q_lat']), g_kv_lat=row(lw['g_kv_lat']),
        g_q_nope=row(lw['g_q_nope']), g_k_nope=row(lw['g_k_nope']),
        g_q_rope2=row(jnp.tile(lw['g_q_rope'], LANES // QK_ROPE)),
        g_k_rope2=row(jnp.tile(lw['g_k_rope'], LANES // QK_ROPE)),
        wc_gdn=lw['w_gdn_conv'].astype(F32), a_log=lw['a_log'].astype(F32), dt_bias=lw['dt_bias'].astype(F32),
        g_gdn_out=row(lw['g_gdn_out']), g_ffn=row(lw['g_ffn_norm']), wc_ffn=lw['w_ffn_conv'].astype(F32),
        bc_ffn=row(lw['b_ffn_conv']))


def _pad_rows_front(a, rows):
    return jnp.pad(a, ((0, 0), (rows - a.shape[1], 0), (0, 0)))


def _prompt_layer(x, w):
    nb, lb, _ = x.shape
    m = nb * lb
    xf = x.reshape(m, D_MODEL)
    proj = _norm_matmul(xf, w['g_attn'], w['w_in'], 1024, 1792)
    cos, sin = _rope_tables(jnp.arange(lb), 1)
    c_kv, k_rope, qf = _mla_pre(proj, cos, sin, w['g_q_lat'], w['g_kv_lat'], w['wq'], w['g_q_nope'],
                                w['g_q_rope2'], w['g_k_rope2'], nb, lb, 512)
    kf, v = _kv_up(c_kv.reshape(nb, lb, KV_LORA), k_rope.reshape(nb, lb, QK_ROPE), w['wk'], w['wv'],
                   w['g_k_nope'], 512)
    o_a = _flash_attention(qf, kf, v, 4096, 1024, 1024).reshape(m, MLA_OUT)

    proj3 = proj.reshape(nb, lb, D_IN_PAD)
    tail0 = jnp.zeros((nb, SUBLANES, GDN_CONV_DIM), F32)
    s0 = jnp.zeros((nb, GDN_HEADS, GDN_DK, GDN_DV), F32)
    o_b, s_new = _gdn(proj3, tail0, s0, w['wc_gdn'], w['a_log'], w['dt_bias'], w['g_gdn_out'])
    x1, h2 = _out_proj(o_a, o_b.reshape(m, GDN_OUT), w['w_out'], xf, w['g_ffn'], 512)

    tm = 1024
    halo0 = jnp.zeros((nb, SUBLANES, D_FF), F32)
    y, gate_tail = _ffn(x1, h2, halo0, w['wg'], w['wu'], w['wc_ffn'], w['bc_ffn'], w['wd'],
                        tm, 512, SUBLANES, 1, lb // tm)
    state = (c_kv.reshape(nb, lb, KV_LORA), k_rope.reshape(nb, lb, QK_ROPE),
             proj3[:, lb - (GDN_CONV - 1):, COL_QKV:COL_QKV + GDN_CONV_DIM], s_new,
             gate_tail.reshape(nb, lb // tm, SUBLANES, D_FF)[:, -1, SUBLANES - (FFN_CONV - 1):, :])
    return y.reshape(nb, lb, D_MODEL), state


def _sample_layer(x, lat_past, krope_past, conv_past, s_past, ffn_past, w):
    nb, lb, _ = x.shape
    past = lat_past.shape[1]
    assert (past + lb - 1) // CHUNK == past // CHUNK and past % CHUNK == 0, "new frames must share one chunk"
    m = nb * lb
    xf = x.reshape(m, D_MODEL)
    proj = _norm_matmul(xf, w['g_attn'], w['w_in'], m, 768)
    cos, sin = _rope_tables(past + jnp.arange(lb), nb)
    c_kv, k_rope, qf = _mla_pre(proj, cos, sin, w['g_q_lat'], w['g_kv_lat'], w['wq'], w['g_q_nope'],
                                w['g_q_rope2'], w['g_k_rope2'], 1, m, m)
    qh = qf[0].reshape(MLA_HEADS, nb, lb, QK_DIM).transpose(1, 0, 2, 3)
    qn_bd = jnp.einsum('bhqd,hg->bhqgd', qh[..., :QK_NOPE], jnp.eye(MLA_HEADS, dtype=qh.dtype))
    qn_bd = qn_bd.reshape(nb, MLA_HEADS * lb, MLA_HEADS * QK_NOPE)
    qr = qh[..., QK_NOPE:].reshape(nb, MLA_HEADS * lb, QK_ROPE)
    o_a = _attn_cached(qn_bd, qr, lat_past, krope_past, c_kv, k_rope, w['wk'], w['wv'], w['g_k_nope'], lb, 1024)

    proj3 = proj.reshape(nb, lb, D_IN_PAD)
    tail0 = _pad_rows_front(conv_past.astype(F32), SUBLANES)
    o_b, s_new = _gdn(proj3, tail0, s_past.astype(F32), w['wc_gdn'], w['a_log'], w['dt_bias'], w['g_gdn_out'])
    x1, h2 = _out_proj(o_a, o_b.reshape(m, GDN_OUT), w['w_out'], xf, w['g_ffn'], m)

    frames_major = lambda a: a.reshape(nb, lb, D_MODEL).transpose(1, 0, 2).reshape(m, D_MODEL)
    x1t, h2t = frames_major(x1), frames_major(h2)
    n_hist = FFN_CONV - 1
    halo0 = ffn_past.astype(F32).transpose(1, 0, 2).reshape(1, n_hist * nb, D_FF)
    yt, gate_tail = _ffn(x1t, h2t, halo0, w['wg'], w['wu'], w['wc_ffn'], w['bc_ffn'], w['wd'],
                         m, 512, n_hist * nb, nb, 1)
    y = yt.reshape(lb, nb, D_MODEL).transpose(1, 0, 2)
    state = (c_kv.reshape(nb, lb, KV_LORA), k_rope.reshape(nb, lb, QK_ROPE),
             proj3[:, lb - (GDN_CONV - 1):, COL_QKV:COL_QKV + GDN_CONV_DIM], s_new,
             gate_tail.reshape(n_hist, nb, D_FF).transpose(1, 0, 2))
    return y, state


def kernel(x_prompt, x_sample, cache_mla_latent, cache_mla_krope, state_gdn_conv, state_gdn_S, state_ffn_conv,
           g_attn_norm, w_in, g_q_lat, g_kv_lat, w_q_up, w_kv_up, g_q_nope, g_q_rope, g_k_nope, g_k_rope,
           w_gdn_conv, a_log, dt_bias, g_gdn_out, w_out, g_ffn_norm, w_ffn_gate, w_ffn_up, w_ffn_conv,
           b_ffn_conv, w_ffn_down):
    xp, xs = x_prompt, x_sample
    new_p, new_s = [], []
    for l in range(w_in.shape[0]):
        w = _prep_weights(dict(
            g_attn_norm=g_attn_norm[l], w_in=w_in[l], g_q_lat=g_q_lat[l], g_kv_lat=g_kv_lat[l], w_q_up=w_q_up[l],
            w_kv_up=w_kv_up[l], g_q_nope=g_q_nope[l], g_q_rope=g_q_rope[l], g_k_nope=g_k_nope[l],
            g_k_rope=g_k_rope[l], w_gdn_conv=w_gdn_conv[l], a_log=a_log[l], dt_bias=dt_bias[l],
            g_gdn_out=g_gdn_out[l], w_out=w_out[l], g_ffn_norm=g_ffn_norm[l], w_ffn_gate=w_ffn_gate[l],
            w_ffn_up=w_ffn_up[l], w_ffn_conv=w_ffn_conv[l], b_ffn_conv=b_ffn_conv[l], w_ffn_down=w_ffn_down[l]))
        xp, st_p = _prompt_layer(xp, w)
        xs, st_s = _sample_layer(xs, cache_mla_latent[l], cache_mla_krope[l], state_gdn_conv[l], state_gdn_S[l],
                                 state_ffn_conv[l], w)
        new_p.append(st_p)
        new_s.append(st_s)
    p_state = [jnp.stack(t) for t in zip(*new_p)]
    s_state = [jnp.stack(t) for t in zip(*new_s)]
    return (xp, xs, *p_state, *s_state)
```

```python
import functools
import math

import jax
import jax.numpy as jnp
import numpy as np
from jax import lax
from jax.experimental import pallas as pl
from jax.experimental.pallas import tpu as pltpu

D_MODEL = 2048
CHUNK = 64
EPS = 1e-6
MLA_HEADS = 8
Q_LORA = 512
KV_LORA = 512
QK_NOPE = 128
QK_ROPE = 64
V_HEAD = 128
ROPE_THETA = 10000.0
GDN_HEADS = 8
GDN_DK = 128
GDN_DV = 128
GDN_CONV = 4
GDN_CONV_DIM = 2 * GDN_HEADS * GDN_DK + GDN_HEADS * GDN_DV
D_FF = 5632
FFN_CONV = 3
MLA_OUT = MLA_HEADS * V_HEAD
GDN_OUT = GDN_HEADS * GDN_DV
QK_DIM = QK_NOPE + QK_ROPE

LANES = 128
SUBLANES = 8
GDN_BLOCK = 128
GDN_BLOCKS_PER_STEP = 2

COL_QKV = 0
COL_Z = COL_QKV + GDN_CONV_DIM
COL_QA = COL_Z + GDN_OUT
COL_KVA = COL_QA + Q_LORA
COL_KR = COL_KVA + KV_LORA
COL_AB = COL_KR + LANES
D_IN_PAD = COL_AB + LANES

BF16 = jnp.bfloat16
F32 = jnp.float32
NT_DIMS = (((1,), (1,)), ((), ()))
TN_DIMS = (((0,), (0,)), ((), ()))


def _params(sem, vmem_mb):
    return pltpu.CompilerParams(dimension_semantics=sem, vmem_limit_bytes=vmem_mb * 1024 * 1024)


def _rms(x, g):
    return x * lax.rsqrt(jnp.mean(x * x, axis=-1, keepdims=True) + EPS) * g


def _sigmoid(x):
    return 1.0 / (1.0 + jnp.exp(-x))


def _softplus(x):
    return jnp.maximum(x, 0.0) + jnp.log(1.0 + jnp.exp(-jnp.abs(x)))


def _norm_matmul_kernel(x_ref, g_ref, w_ref, o_ref, h_ref):
    @pl.when(pl.program_id(1) == 0)
    def _():
        h_ref[...] = _rms(x_ref[...], g_ref[...]).astype(BF16)

    o_ref[...] = lax.dot_general(h_ref[...], w_ref[...], NT_DIMS, preferred_element_type=F32)


def _norm_matmul(x, g, wt, tm, tn):
    m, k = x.shape
    n = wt.shape[0]
    return pl.pallas_call(
        _norm_matmul_kernel,
        grid=(m // tm, n // tn),
        in_specs=[pl.BlockSpec((tm, k), lambda i, j: (i, 0)),
                  pl.BlockSpec((1, k), lambda i, j: (0, 0)),
                  pl.BlockSpec((tn, k), lambda i, j: (j, 0))],
        out_specs=pl.BlockSpec((tm, tn), lambda i, j: (i, j)),
        out_shape=jax.ShapeDtypeStruct((m, n), F32),
        scratch_shapes=[pltpu.VMEM((tm, k), BF16)],
        compiler_params=_params(("arbitrary", "arbitrary"), 56),
        name="in_proj",
    )(x, g, wt)


def _rope_pairs(y, cos, sin):
    lane = lax.broadcasted_iota(jnp.int32, (1, LANES), 1)
    first_half = (lane % QK_ROPE) < (QK_ROPE // 2)
    swapped = jnp.where(first_half, pltpu.roll(y, LANES - QK_ROPE // 2, 1), pltpu.roll(y, QK_ROPE // 2, 1))
    return y * cos + swapped * sin


def _mla_pre_kernel(qa_ref, kva_ref, kr_ref, cos_ref, sin_ref, gq_ref, gkv_ref, wq_ref, gqn_ref, gqr_ref,
                    gkr_ref, ckv_ref, krope_ref, qf_ref, *, scale):
    cos = cos_ref[...]
    sin = sin_ref[...]
    lane = lax.broadcasted_iota(jnp.int32, (1, LANES), 1)
    lo = lane < QK_ROPE

    ckv_ref[...] = _rms(kva_ref[...], gkv_ref[...])

    kr = kr_ref[...]
    ss = jnp.sum(jnp.where(lo, kr * kr, 0.0), axis=-1, keepdims=True)
    kr = kr * lax.rsqrt(ss * (1.0 / QK_ROPE) + EPS) * gkr_ref[...]
    krope_ref[...] = _rope_pairs(kr, cos, sin)[:, :QK_ROPE]

    hq = _rms(qa_ref[...], gq_ref[...]).astype(BF16)
    q = jnp.dot(hq, wq_ref[...], preferred_element_type=F32)
    gqn = gqn_ref[...] * scale
    for h in range(MLA_HEADS):
        xn = q[:, h * QK_NOPE:(h + 1) * QK_NOPE]
        qf_ref[0, h, :, 0:QK_NOPE] = (_rms(xn, gqn)).astype(BF16)
    gqr = gqr_ref[...] * scale
    rope0 = MLA_HEADS * QK_NOPE
    for p in range(MLA_HEADS // 2):
        xr = q[:, rope0 + p * LANES: rope0 + (p + 1) * LANES]
        sq = xr * xr
        s_lo = jnp.sum(jnp.where(lo, sq, 0.0), axis=-1, keepdims=True)
        s_hi = jnp.sum(jnp.where(lo, 0.0, sq), axis=-1, keepdims=True)
        r = jnp.where(lo, lax.rsqrt(s_lo * (1.0 / QK_ROPE) + EPS), lax.rsqrt(s_hi * (1.0 / QK_ROPE) + EPS))
        ro = _rope_pairs(xr * r * gqr, cos, sin).astype(BF16)
        qf_ref[0, 2 * p, :, QK_NOPE:QK_DIM] = ro[:, :QK_ROPE]
        qf_ref[0, 2 * p + 1, :, QK_NOPE:QK_DIM] = ro[:, QK_ROPE:]


def _mla_pre(proj, cos, sin, g_q_lat, g_kv_lat, wq, g_q_nope, g_q_rope2, g_k_rope2, nb, lb, tm):
    m = proj.shape[0]
    per_seq = lb // tm
    n_tab = cos.shape[0] // tm
    scale = float(QK_DIM) ** -0.5 * math.log2(math.e)
    row = lambda i: (i, 0)
    const = lambda i: (0, 0)
    return pl.pallas_call(
        functools.partial(_mla_pre_kernel, scale=scale),
        grid=(m // tm,),
        in_specs=[pl.BlockSpec((tm, Q_LORA), lambda i: (i, COL_QA // Q_LORA)),
                  pl.BlockSpec((tm, KV_LORA), lambda i: (i, COL_KVA // KV_LORA)),
                  pl.BlockSpec((tm, LANES), lambda i: (i, COL_KR // LANES)),
                  pl.BlockSpec((tm, LANES), lambda i: (i % n_tab, 0)),
                  pl.BlockSpec((tm, LANES), lambda i: (i % n_tab, 0)),
                  pl.BlockSpec((1, Q_LORA), const),
                  pl.BlockSpec((1, KV_LORA), const),
                  pl.BlockSpec(wq.shape, const),
                  pl.BlockSpec((1, QK_NOPE), const),
                  pl.BlockSpec((1, LANES), const),
                  pl.BlockSpec((1, LANES), const)],
        out_specs=[pl.BlockSpec((tm, KV_LORA), row),
                   pl.BlockSpec((tm, QK_ROPE), row),
                   pl.BlockSpec((1, MLA_HEADS, tm, QK_DIM), lambda i: (i // per_seq, 0, i % per_seq, 0))],
        out_shape=[jax.ShapeDtypeStruct((m, KV_LORA), F32),
                   jax.ShapeDtypeStruct((m, QK_ROPE), F32),
                   jax.ShapeDtypeStruct((nb, MLA_HEADS, lb, QK_DIM), BF16)],
        compiler_params=_params(("arbitrary",), 40),
        name="mla_pre",
    )(proj, proj, proj, cos, sin, g_q_lat, g_kv_lat, wq, g_q_nope, g_q_rope2, g_k_rope2)


def _kv_up_kernel(lat_ref, kr_ref, wk_ref, wv_ref, gk_ref, kf_ref, v_ref):
    lat = lat_ref[0].astype(BF16)
    kn = jnp.dot(lat, wk_ref[...], preferred_element_type=F32)
    vv = jnp.dot(lat, wv_ref[...], preferred_element_type=F32)
    kr = kr_ref[0].astype(BF16)
    gk = gk_ref[...]
    for h in range(MLA_HEADS):
        kf_ref[0, h, :, 0:QK_NOPE] = _rms(kn[:, h * QK_NOPE:(h + 1) * QK_NOPE], gk).astype(BF16)
        kf_ref[0, h, :, QK_NOPE:QK_DIM] = kr
        v_ref[0, h] = vv[:, h * V_HEAD:(h + 1) * V_HEAD].astype(BF16)


def _kv_up(lat, krope, wk, wv, g_k_nope, tm):
    nb, t, _ = lat.shape
    const = lambda b, i: (0, 0)
    return pl.pallas_call(
        _kv_up_kernel,
        grid=(nb, t // tm),
        in_specs=[pl.BlockSpec((1, tm, KV_LORA), lambda b, i: (b, i, 0)),
                  pl.BlockSpec((1, tm, QK_ROPE), lambda b, i: (b, i, 0)),
                  pl.BlockSpec(wk.shape, const),
                  pl.BlockSpec(wv.shape, const),
                  pl.BlockSpec((1, QK_NOPE), const)],
        out_specs=[pl.BlockSpec((1, MLA_HEADS, tm, QK_DIM), lambda b, i: (b, 0, i, 0)),
                   pl.BlockSpec((1, MLA_HEADS, tm, V_HEAD), lambda b, i: (b, 0, i, 0))],
        out_shape=[jax.ShapeDtypeStruct((nb, MLA_HEADS, t, QK_DIM), BF16),
                   jax.ShapeDtypeStruct((nb, MLA_HEADS, t, V_HEAD), BF16)],
        compiler_params=_params(("arbitrary", "arbitrary"), 40),
        name="kv_up",
    )(lat, krope, wk, wv, g_k_nope)


def _flash_kernel(q_ref, k_ref, v_ref, o_ref, m_ref, l_ref, acc_ref, *, tq, tk, sub):
    qi = pl.program_id(2)
    m_ref[...] = jnp.full(m_ref.shape, -jnp.inf, F32)
    l_ref[...] = jnp.zeros(l_ref.shape, F32)
    acc_ref[...] = jnp.zeros(acc_ref.shape, F32)

    def attend_rows(r0, nr, keys, k0, masked):
        rows = pl.ds(r0, nr)
        s = lax.dot_general(q_ref[0, 0, rows, :], k_ref[0, 0, keys, :], NT_DIMS, preferred_element_type=F32)
        if masked:
            rc = (r0 + lax.broadcasted_iota(jnp.int32, (nr, tk), 0)) // CHUNK
            cc = (k0 + lax.broadcasted_iota(jnp.int32, (nr, tk), 1)) // CHUNK
            s = jnp.where(cc <= rc, s, -jnp.inf)
        m = m_ref[rows, :]
        m_new = jnp.maximum(m, jnp.max(s, axis=-1, keepdims=True))
        alpha = jnp.exp2(m - m_new)
        pc = [jnp.exp2(s[:, c * LANES:(c + 1) * LANES] - m_new) for c in range(tk // LANES)]
        psum = pc[0]
        for c in range(1, tk // LANES):
            psum = psum + pc[c]
        p = jnp.concatenate(pc, axis=1)
        m_ref[rows, :] = m_new
        l_ref[rows, :] = alpha * l_ref[rows, :] + psum
        acc_ref[rows, :] = alpha * acc_ref[rows, :] + jnp.dot(p.astype(BF16), v_ref[0, 0, keys, :],
                                                              preferred_element_type=F32)

    def attend(r_lo, r_hi, ki, k0, masked):
        keys = pl.ds(pl.multiple_of(ki * tk, tk), tk)
        for r0 in range(r_lo, r_hi, sub):
            if not masked or r0 + sub > k0:
                attend_rows(r0, sub, keys, k0, masked and r0 < k0 + tk)

    def body(ki, c):
        attend(0, tq, ki, 0, False)
        return c

    n_diag = tq // tk
    lax.fori_loop(0, n_diag * qi, body, 0)
    for j in range(n_diag):
        attend(0, tq, n_diag * qi + j, j * tk, True)
    l = jnp.sum(l_ref[...], axis=-1, keepdims=True)
    o_ref[0] = (acc_ref[...] / l).astype(BF16)


def _flash_attention(qf, kf, v, tq, tk, sub):
    nb, nh, lq, _ = qf.shape
    t = kf.shape[2]
    return pl.pallas_call(
        functools.partial(_flash_kernel, tq=tq, tk=tk, sub=sub),
        grid=(nb, nh, lq // tq),
        in_specs=[pl.BlockSpec((1, 1, tq, QK_DIM), lambda b, h, i: (b, h, i, 0)),
                  pl.BlockSpec((1, 1, t, QK_DIM), lambda b, h, i: (b, h, 0, 0)),
                  pl.BlockSpec((1, 1, t, V_HEAD), lambda b, h, i: (b, h, 0, 0))],
        out_specs=pl.BlockSpec((1, tq, V_HEAD), lambda b, h, i: (b, i, h)),
        out_shape=jax.ShapeDtypeStruct((nb, lq, nh * V_HEAD), BF16),
        scratch_shapes=[pltpu.VMEM((tq, LANES), F32), pltpu.VMEM((tq, LANES), F32),
                        pltpu.VMEM((tq, V_HEAD), F32)],
        compiler_params=_params(("arbitrary", "arbitrary", "arbitrary"), 40),
        name="flash_attn",
    )(qf, kf, v)


def _attn_cached_kernel(qn_ref, qr_ref, lat_ref, kr_ref, latn_ref, krn_ref, wk_ref, wv_ref, gk_ref, o_ref,
                        m_ref, l_ref, acc_ref, *, lq):
    kt = pl.program_id(1)

    @pl.when(kt == 0)
    def _():
        m_ref[...] = jnp.full(m_ref.shape, -jnp.inf, F32)
        l_ref[...] = jnp.zeros(l_ref.shape, F32)
        acc_ref[...] = jnp.zeros(acc_ref.shape, F32)

    def attend(lat, kr_t):
        latb = lat.astype(BF16)
        kn = jnp.dot(latb, wk_ref[...], preferred_element_type=F32)
        gk = gk_ref[...]
        knb = jnp.concatenate([_rms(kn[:, h * QK_NOPE:(h + 1) * QK_NOPE], gk).astype(BF16)
                               for h in range(MLA_HEADS)], axis=1)
        s = (lax.dot_general(qn_ref[0], knb, NT_DIMS, preferred_element_type=F32)
             + jnp.dot(qr_ref[0], kr_t.astype(BF16), preferred_element_type=F32))
        m = m_ref[...]
        m_new = jnp.maximum(m, jnp.max(s, axis=-1, keepdims=True))
        alpha = jnp.exp2(m - m_new)
        p = jnp.exp2(s - m_new[:, 0:1])
        m_ref[...] = m_new
        l_ref[...] = alpha * l_ref[...] + jnp.sum(p, axis=-1, keepdims=True)
        acc_ref[...] = alpha[:, 0:1] * acc_ref[...] + jnp.dot(p.astype(BF16), latb, preferred_element_type=F32)

    attend(lat_ref[0], kr_ref[0])

    @pl.when(kt == pl.num_programs(1) - 1)
    def _():
        attend(latn_ref[...], krn_ref[0])
        o_lat = (acc_ref[...] / l_ref[:, 0:1]).astype(BF16)
        for h in range(MLA_HEADS):
            o_ref[:, h * V_HEAD:(h + 1) * V_HEAD] = jnp.dot(
                o_lat[h * lq:(h + 1) * lq, :], wv_ref[:, h * V_HEAD:(h + 1) * V_HEAD],
                preferred_element_type=F32).astype(BF16)


def _attn_cached(qn_bd, qr, lat_past, kr_past, lat_new, kr_new, wk, wv, g_k_nope, lq, tk):
    nb, past, _ = lat_past.shape
    nrow = MLA_HEADS * lq
    kr_past_t = kr_past.transpose(0, 2, 1)
    kr_new_t = kr_new.reshape(nb, lq, QK_ROPE).transpose(0, 2, 1)
    per_b = lambda b, k: (b, 0, 0)
    tile = lambda b, k: (b, k, 0)
    new = lambda b, k: (b, 0)
    const = lambda b, k: (0, 0)
    return pl.pallas_call(
        functools.partial(_attn_cached_kernel, lq=lq),
        grid=(nb, past // tk),
        in_specs=[pl.BlockSpec((1, nrow, MLA_HEADS * QK_NOPE), per_b),
                  pl.BlockSpec((1, nrow, QK_ROPE), per_b),
                  pl.BlockSpec((1, tk, KV_LORA), tile),
                  pl.BlockSpec((1, QK_ROPE, tk), lambda b, k: (b, 0, k)),
                  pl.BlockSpec((lq, KV_LORA), new),
                  pl.BlockSpec((1, QK_ROPE, lq), per_b),
                  pl.BlockSpec(wk.shape, const),
                  pl.BlockSpec(wv.shape, const),
                  pl.BlockSpec((1, QK_NOPE), const)],
        out_specs=pl.BlockSpec((lq, MLA_HEADS * V_HEAD), new),
        out_shape=jax.ShapeDtypeStruct((nb * lq, MLA_HEADS * V_HEAD), BF16),
        scratch_shapes=[pltpu.VMEM((nrow, LANES), F32), pltpu.VMEM((nrow, LANES), F32),
                        pltpu.VMEM((nrow, KV_LORA), F32)],
        compiler_params=_params(("arbitrary", "arbitrary"), 40),
        name="attn_cached",
    )(qn_bd, qr, lat_past, kr_past_t, lat_new, kr_new_t, wk, wv, g_k_nope)


def _gdn_kernel(qkv_ref, z_ref, abc_ref, abr_ref, tail0_ref, s0_ref, wc_ref, alog_c_ref, dt_c_ref, alog_r_ref,
                dt_r_ref, gout_ref, o_ref, sfin_ref, ext_ref, s_ref, *, blk, nsub, valid, n_levels):
    t = pl.program_id(1)
    nt = pl.num_programs(1)
    halo = SUBLANES
    rows = nsub * blk
    n_real = rows if valid == blk else valid

    @pl.when(t == 0)
    def _():
        ext_ref[0:halo, :] = tail0_ref[0]
        s_ref[...] = s0_ref[0]

    @pl.when(t > 0)
    def _():
        ext_ref[0:halo, :] = ext_ref[rows:rows + halo, :]

    def pad_rows(x, to=rows):
        return x if x.shape[0] == to else jnp.concatenate(
            [x, jnp.zeros((to - x.shape[0], x.shape[1]), x.dtype)], axis=0)

    conv_rows = -(-n_real // SUBLANES) * SUBLANES
    ext_ref[halo:halo + conv_rows, :] = pad_rows(qkv_ref[0], conv_rows)
    wc = wc_ref[...]
    conv = wc[GDN_CONV - 1:GDN_CONV, :] * ext_ref[halo:halo + conv_rows, :]
    for i in range(1, GDN_CONV):
        conv = conv + wc[GDN_CONV - 1 - i:GDN_CONV - i, :] * ext_ref[halo - i:halo - i + conv_rows, :]
    act = pad_rows(conv * _sigmoid(conv))

    abc = pad_rows(abc_ref[0])
    abr = abr_ref[0, 0]
    rvalid = lax.broadcasted_iota(jnp.int32, (rows, 1), 0) < n_real
    cvalid = lax.broadcasted_iota(jnp.int32, (1, rows), 1) < n_real
    g_col = jnp.where(rvalid, -jnp.exp(alog_c_ref[...]) * _softplus(abc + dt_c_ref[...]), 0.0)
    beta_col = jnp.where(rvalid, _sigmoid(abc), 0.0)
    g_row = jnp.where(cvalid, -jnp.exp(alog_r_ref[...]) * _softplus(abr + dt_r_ref[...]), 0.0)
    ii = lax.broadcasted_iota(jnp.int32, (blk, blk), 0)
    jj = lax.broadcasted_iota(jnp.int32, (blk, blk), 1)
    incl = ii >= jj
    lower = incl.astype(F32)
    upper = (ii <= jj).astype(F32)
    gc_cols = [jnp.dot(lower, g_col[sb * blk:(sb + 1) * blk], preferred_element_type=F32,
                       precision=lax.Precision.HIGHEST) for sb in range(nsub)]
    gc_rows = [jnp.dot(g_row[:, sb * blk:(sb + 1) * blk], upper, preferred_element_type=F32,
                       precision=lax.Precision.HIGHEST) for sb in range(nsub)]
    eye = (ii == jj).astype(F32)
    merge_masks = []
    for lvl in range(n_levels):
        half = 1 << lvl
        merge_masks.append((ii // (2 * half) == jj // (2 * half)) & ((ii // half) % 2 == 1)
                           & ((jj // half) % 2 == 0))
    nk = GDN_HEADS * GDN_DK
    gout = gout_ref[...]

    heads = range(GDN_HEADS)
    units = [(sb, h) for sb in range(nsub) for h in heads]
    s_cur = [s_ref[h] for h in heads]
    n_out = blk if n_real == rows else n_real
    zs = {(sb, h): z_ref[0, sb * blk:sb * blk + n_out, h * GDN_DV:(h + 1) * GDN_DV] for sb, h in units}
    q, k, v, gc, beta, decay = {}, {}, {}, {}, {}, {}
    for sb, h in units:
        rs = slice(sb * blk, (sb + 1) * blk)
        qh = act[rs, h * GDN_DK:(h + 1) * GDN_DK]
        kh = act[rs, nk + h * GDN_DK: nk + (h + 1) * GDN_DK]
        u = (sb, h)
        q[u] = qh * lax.rsqrt(jnp.sum(qh * qh, axis=-1, keepdims=True) + EPS) * (float(GDN_DK) ** -0.5)
        k[u] = kh * lax.rsqrt(jnp.sum(kh * kh, axis=-1, keepdims=True) + EPS)
        v[u] = act[rs, 2 * nk + h * GDN_DV: 2 * nk + (h + 1) * GDN_DV]
        gc[u] = gc_cols[sb][:, h:h + 1]
        beta[u] = beta_col[rs, GDN_HEADS + h:GDN_HEADS + h + 1]
        decay[u] = jnp.where(incl, jnp.exp(gc[u] - gc_rows[sb][h:h + 1, :]), 0.0)
    kb = {u: k[u].astype(BF16) for u in units}
    kq = {u: lax.dot_general(jnp.concatenate([kb[u], q[u].astype(BF16)], axis=0), kb[u], NT_DIMS,
                             preferred_element_type=F32) for u in units}
    a = {u: beta[u] * kq[u][0:blk] * decay[u] for u in units}
    tinv = {u: eye - jnp.where(merge_masks[0], a[u], 0.0) for u in units}
    for lvl in range(1, n_levels):
        tb = {u: tinv[u].astype(BF16) for u in units}
        y = {u: jnp.dot(jnp.where(merge_masks[lvl], a[u], 0.0).astype(BF16), tb[u], preferred_element_type=F32)
             for u in units}
        tinv = {u: tinv[u] - jnp.dot(tb[u], y[u].astype(BF16), preferred_element_type=F32) for u in units}
    egc = {u: jnp.exp(gc[u]) for u in units}
    uw = {u: jnp.dot(tinv[u].astype(BF16),
                     jnp.concatenate([v[u] * beta[u], k[u] * (beta[u] * egc[u])], axis=1).astype(BF16),
                     preferred_element_type=F32) for u in units}
    o = {}
    for sb in range(nsub):
        sbf = [x.astype(BF16) for x in s_cur]
        ws = [jnp.dot(jnp.concatenate([uw[sb, h][:, GDN_DV:], q[sb, h] * egc[sb, h]], axis=0).astype(BF16),
                      sbf[h], preferred_element_type=F32) for h in heads]
        vb = [(uw[sb, h][:, 0:GDN_DV] - ws[h][0:blk]).astype(BF16) for h in heads]
        g_last = [gc[sb, h][blk - 1:blk, :] for h in heads]
        for h in heads:
            o[sb, h] = ws[h][blk:2 * blk] + jnp.dot((kq[sb, h][blk:2 * blk] * decay[sb, h]).astype(BF16), vb[h],
                                                    preferred_element_type=F32)
        s_cur = [s_cur[h] * jnp.exp(g_last[h])
                 + lax.dot_general((k[sb, h] * jnp.exp(g_last[h] - gc[sb, h])).astype(BF16), vb[h], TN_DIMS,
                                   preferred_element_type=F32) for h in heads]
    for h in heads:
        s_ref[h] = s_cur[h]
    for sb, h in units:
        o_ref[0, sb * blk:sb * blk + n_out, h * GDN_DV:(h + 1) * GDN_DV] = (
            _rms(o[sb, h][0:n_out], gout) * (zs[sb, h] * _sigmoid(zs[sb, h]))).astype(BF16)

    @pl.when(t == nt - 1)
    def _():
        sfin_ref[0] = s_ref[...]


def _gdn(proj3, tail0, s0, wc, a_log, dt_bias, g_out):
    nb, t, _ = proj3.shape
    blk = GDN_BLOCK
    valid = min(t, blk)
    nsub = GDN_BLOCKS_PER_STEP if t % (GDN_BLOCKS_PER_STEP * blk) == 0 else 1
    rin = nsub * valid
    rows = nsub * blk
    ab = proj3[:, :, COL_AB:COL_AB + 2 * GDN_HEADS].reshape(nb, t // rin, rin, 2 * GDN_HEADS)
    ab_rows = jnp.pad(ab.transpose(0, 1, 3, 2), ((0, 0), (0, 0), (0, 0), (0, rows - rin)))
    pad_c = lambda v: jnp.zeros((1, LANES), F32).at[0, :GDN_HEADS].set(v)
    pad_r = lambda v: jnp.zeros((2 * GDN_HEADS, 1), F32).at[:GDN_HEADS, 0].set(v)
    const2 = lambda b, i: (0, 0)
    return pl.pallas_call(
        functools.partial(_gdn_kernel, blk=blk, nsub=nsub, valid=valid, n_levels=int(math.log2(blk))),
        grid=(nb, t // rin),
        in_specs=[pl.BlockSpec((1, rin, GDN_CONV_DIM), lambda b, i: (b, i, COL_QKV // GDN_CONV_DIM)),
                  pl.BlockSpec((1, rin, GDN_OUT), lambda b, i: (b, i, COL_Z // GDN_OUT)),
                  pl.BlockSpec((1, rin, LANES), lambda b, i: (b, i, COL_AB // LANES)),
                  pl.BlockSpec((1, 1, 2 * GDN_HEADS, rows), lambda b, i: (b, i, 0, 0)),
                  pl.BlockSpec((1, SUBLANES, GDN_CONV_DIM), lambda b, i: (b, 0, 0)),
                  pl.BlockSpec((1, GDN_HEADS, GDN_DK, GDN_DV), lambda b, i: (b, 0, 0, 0)),
                  pl.BlockSpec((GDN_CONV, GDN_CONV_DIM), const2),
                  pl.BlockSpec((1, LANES), const2),
                  pl.BlockSpec((1, LANES), const2),
                  pl.BlockSpec((2 * GDN_HEADS, 1), const2),
                  pl.BlockSpec((2 * GDN_HEADS, 1), const2),
                  pl.BlockSpec((1, GDN_DV), const2)],
        out_specs=[pl.BlockSpec((1, rin, GDN_OUT), lambda b, i: (b, i, 0)),
                   pl.BlockSpec((1, GDN_HEADS, GDN_DK, GDN_DV), lambda b, i: (b, 0, 0, 0))],
        out_shape=[jax.ShapeDtypeStruct((nb, t, GDN_OUT), BF16),
                   jax.ShapeDtypeStruct((nb, GDN_HEADS, GDN_DK, GDN_DV), F32)],
        scratch_shapes=[pltpu.VMEM((rows + 2 * SUBLANES, GDN_CONV_DIM), F32),
                        pltpu.VMEM((GDN_HEADS, GDN_DK, GDN_DV), F32)],
        compiler_params=_params(("arbitrary", "arbitrary"), 40),
        name="gdn",
    )(proj3, proj3, proj3, ab_rows, tail0, s0, wc, pad_c(a_log), pad_c(dt_bias), pad_r(a_log), pad_r(dt_bias),
      g_out)


def _out_proj_kernel(oa_ref, ob_ref, w_ref, x_ref, y_ref):
    y_ref[...] = (x_ref[...]
                  + jnp.dot(oa_ref[...], w_ref[0:MLA_OUT, :], preferred_element_type=F32)
                  + jnp.dot(ob_ref[...], w_ref[MLA_OUT:MLA_OUT + GDN_OUT, :], preferred_element_type=F32))


def _out_proj(o_a, o_b, w, x, tm):
    m = x.shape[0]
    row = lambda i: (i, 0)
    return pl.pallas_call(
        _out_proj_kernel,
        grid=(m // tm,),
        in_specs=[pl.BlockSpec((tm, MLA_OUT), row),
                  pl.BlockSpec((tm, GDN_OUT), row),
                  pl.BlockSpec(w.shape, lambda i: (0, 0)),
                  pl.BlockSpec((tm, D_MODEL), row)],
        out_specs=pl.BlockSpec((tm, D_MODEL), row),
        out_shape=jax.ShapeDtypeStruct((m, D_MODEL), F32),
        compiler_params=_params(("arbitrary",), 48),
        name="out_proj",
    )(o_a, o_b, w, x)


def _ffn_kernel(x_ref, g_ref, halo0_ref, wg_ref, wu_ref, wc_ref, bc_ref, wd_ref, y_ref, st_ref, h_ref, ext_ref,
                carry_ref, *, tm, rows, halo, shift, per_seq):
    i = pl.program_id(0)
    f = pl.program_id(1)

    @pl.when(f == 0)
    def _():
        x = x_ref[...]
        h_ref[...] = _rms(x, g_ref[...]).astype(BF16)
        y_ref[...] = x

    @pl.when(i % per_seq == 0)
    def _():
        ext_ref[0:halo, :] = halo0_ref[0]

    @pl.when(i % per_seq != 0)
    def _():
        ext_ref[0:halo, :] = carry_ref[f]

    wc = wc_ref[...]
    bc = bc_ref[...]
    for r0 in range(0, tm, rows):
        h = h_ref[r0:r0 + rows, :]
        gate = jnp.dot(h, wg_ref[...], preferred_element_type=F32)
        up = jnp.dot(h, wu_ref[...], preferred_element_type=F32)
        ext_ref[halo + r0:halo + r0 + rows, :] = gate
        gc = (wc[2:3, :] * gate + wc[1:2, :] * ext_ref[halo + r0 - shift:halo + r0 - shift + rows, :]
              + wc[0:1, :] * ext_ref[halo + r0 - 2 * shift:halo + r0 - 2 * shift + rows, :] + bc)
        act = (gc * _sigmoid(gc)) * up
        y_ref[r0:r0 + rows, :] += jnp.dot(act.astype(BF16), wd_ref[...], preferred_element_type=F32)
    last = ext_ref[tm:tm + halo, :]
    carry_ref[f] = last
    st_ref[0] = last


def _ffn(x, g, halo0, wg, wu, wc, bc, wd, tm, tf, halo, shift, per_seq):
    m = x.shape[0]
    nf = D_FF // tf
    return pl.pallas_call(
        functools.partial(_ffn_kernel, tm=tm, rows=min(tm, 512), halo=halo, shift=shift, per_seq=per_seq),
        grid=(m // tm, nf),
        in_specs=[pl.BlockSpec((tm, D_MODEL), lambda i, f: (i, 0), pipeline_mode=pl.Buffered(1)),
                  pl.BlockSpec((1, D_MODEL), lambda i, f: (0, 0)),
                  pl.BlockSpec((1, halo, tf), lambda i, f: (i // per_seq, 0, f)),
                  pl.BlockSpec((D_MODEL, tf), lambda i, f: (0, f)),
                  pl.BlockSpec((D_MODEL, tf), lambda i, f: (0, f)),
                  pl.BlockSpec((FFN_CONV, tf), lambda i, f: (0, f)),
                  pl.BlockSpec((1, tf), lambda i, f: (0, f)),
                  pl.BlockSpec((tf, D_MODEL), lambda i, f: (f, 0))],
        out_specs=[pl.BlockSpec((tm, D_MODEL), lambda i, f: (i, 0)),
                   pl.BlockSpec((1, halo, tf), lambda i, f: (i, 0, f))],
        out_shape=[jax.ShapeDtypeStruct((m, D_MODEL), F32),
                   jax.ShapeDtypeStruct((m // tm, halo, D_FF), F32)],
        scratch_shapes=[pltpu.VMEM((tm, D_MODEL), BF16),
                        pltpu.VMEM((halo + tm, tf), F32),
                        pltpu.VMEM((nf, halo, tf), F32)],
        compiler_params=_params(("arbitrary", "arbitrary"), 60),
        name="conv_ffn",
    )(x, g, halo0, wg, wu, wc, bc, wd)


def _rope_tables(pos, reps):
    half = QK_ROPE // 2
    inv = 1.0 / (ROPE_THETA ** (jnp.arange(half, dtype=F32) / half))
    ang = pos.astype(F32)[:, None] * inv[None, :]
    cos, sin = jnp.cos(ang), jnp.sin(ang)
    cos = jnp.tile(jnp.concatenate([cos, cos], axis=-1), (reps, LANES // QK_ROPE))
    sin = jnp.tile(jnp.concatenate([-sin, sin], axis=-1), (reps, LANES // QK_ROPE))
    return cos, sin


def _prep_weights(lw):
    w_in = lw['w_in']
    off = np.cumsum([Q_LORA, KV_LORA, QK_ROPE, GDN_CONV_DIM, GDN_OUT, GDN_HEADS, GDN_HEADS]).tolist()
    wt = w_in.T
    zr = lambda n: jnp.zeros((n, D_MODEL), w_in.dtype)
    w_in_r = jnp.concatenate([wt[off[2]:off[4]], wt[:off[1]], wt[off[1]:off[2]], zr(LANES - QK_ROPE),
                              wt[off[4]:off[6]], zr(LANES - 2 * GDN_HEADS)], axis=0)
    wq = lw['w_q_up'].reshape(Q_LORA, MLA_HEADS, QK_DIM)
    wq_r = jnp.concatenate([wq[:, :, :QK_NOPE].reshape(Q_LORA, -1), wq[:, :, QK_NOPE:].reshape(Q_LORA, -1)], axis=1)
    wkv = lw['w_kv_up']
    row = lambda v: v.reshape(1, -1).astype(F32)
    return dict(
        w_in=w_in_r.astype(BF16), wq=wq_r.astype(BF16),
        wk=wkv[:, :, :QK_NOPE].reshape(KV_LORA, -1).astype(BF16),
        wv=wkv[:, :, QK_NOPE:].reshape(KV_LORA, -1).astype(BF16),
        w_out=lw['w_out'].astype(BF16), wg=lw['w_ffn_gate'].astype(BF16), wu=lw['w_ffn_up'].astype(BF16),
        wd=lw['w_ffn_down'].astype(BF16),
        g_attn=row(lw['g_attn_norm']), g_q_lat=row(lw['g_q_lat']), g_kv_lat=row(lw['g_kv_lat']),
        g_q_nope=row(lw['g_q_nope']), g_k_nope=row(lw['g_k_nope']),
        g_q_rope2=row(jnp.tile(lw['g_q_rope'], LANES // QK_ROPE)),
        g_k_rope2=row(jnp.tile(lw['g_k_rope'], LANES // QK_ROPE)),
        wc_gdn=lw['w_gdn_conv'].astype(F32), a_log=lw['a_log'].astype(F32), dt_bias=lw['dt_bias'].astype(F32),
        g_gdn_out=row(lw['g_gdn_out']), g_ffn=row(lw['g_ffn_norm']), wc_ffn=lw['w_ffn_conv'].astype(F32),
        bc_ffn=row(lw['b_ffn_conv']))


def _pad_rows_front(a, rows):
    return jnp.pad(a, ((0, 0), (rows - a.shape[1], 0), (0, 0)))


def _prompt_layer(x, w):
    nb, lb, _ = x.shape
    m = nb * lb
    xf = x.reshape(m, D_MODEL)
    proj = _norm_matmul(xf, w['g_attn'], w['w_in'], 1024, 1792)
    cos, sin = _rope_tables(jnp.arange(lb), 1)
    c_kv, k_rope, qf = _mla_pre(proj, cos, sin, w['g_q_lat'], w['g_kv_lat'], w['wq'], w['g_q_nope'],
                                w['g_q_rope2'], w['g_k_rope2'], nb, lb, 512)
    kf, v = _kv_up(c_kv.reshape(nb, lb, KV_LORA), k_rope.reshape(nb, lb, QK_ROPE), w['wk'], w['wv'],
                   w['g_k_nope'], 512)
    o_a = _flash_attention(qf, kf, v, 4096, 1024, 1024).reshape(m, MLA_OUT)

    proj3 = proj.reshape(nb, lb, D_IN_PAD)
    tail0 = jnp.zeros((nb, SUBLANES, GDN_CONV_DIM), F32)
    s0 = jnp.zeros((nb, GDN_HEADS, GDN_DK, GDN_DV), F32)
    o_b, s_new = _gdn(proj3, tail0, s0, w['wc_gdn'], w['a_log'], w['dt_bias'], w['g_gdn_out'])
    x1 = _out_proj(o_a, o_b.reshape(m, GDN_OUT), w['w_out'], xf, 512)

    tm = 1024
    halo0 = jnp.zeros((nb, SUBLANES, D_FF), F32)
    y, gate_tail = _ffn(x1, w['g_ffn'], halo0, w['wg'], w['wu'], w['wc_ffn'], w['bc_ffn'], w['wd'],
                        tm, 512, SUBLANES, 1, lb // tm)
    state = (c_kv.reshape(nb, lb, KV_LORA), k_rope.reshape(nb, lb, QK_ROPE),
             proj3[:, lb - (GDN_CONV - 1):, COL_QKV:COL_QKV + GDN_CONV_DIM], s_new,
             gate_tail.reshape(nb, lb // tm, SUBLANES, D_FF)[:, -1, SUBLANES - (FFN_CONV - 1):, :])
    return y.reshape(nb, lb, D_MODEL), state


def _sample_layer(x, lat_past, krope_past, conv_past, s_past, ffn_past, w):
    nb, lb, _ = x.shape
    past = lat_past.shape[1]
    assert (past + lb - 1) // CHUNK == past // CHUNK and past % CHUNK == 0, "new frames must share one chunk"
    m = nb * lb
    xf = x.reshape(m, D_MODEL)
    proj = _norm_matmul(xf, w['g_attn'], w['w_in'], m, 768)
    cos, sin = _rope_tables(past + jnp.arange(lb), nb)
    c_kv, k_rope, qf = _mla_pre(proj, cos, sin, w['g_q_lat'], w['g_kv_lat'], w['wq'], w['g_q_nope'],
                                w['g_q_rope2'], w['g_k_rope2'], 1, m, m)
    qh = qf[0].reshape(MLA_HEADS, nb, lb, QK_DIM).transpose(1, 0, 2, 3)
    qn_bd = jnp.einsum('bhqd,hg->bhqgd', qh[..., :QK_NOPE], jnp.eye(MLA_HEADS, dtype=qh.dtype))
    qn_bd = qn_bd.reshape(nb, MLA_HEADS * lb, MLA_HEADS * QK_NOPE)
    qr = qh[..., QK_NOPE:].reshape(nb, MLA_HEADS * lb, QK_ROPE)
    o_a = _attn_cached(qn_bd, qr, lat_past, krope_past, c_kv, k_rope, w['wk'], w['wv'], w['g_k_nope'], lb, 1024)

    proj3 = proj.reshape(nb, lb, D_IN_PAD)
    tail0 = _pad_rows_front(conv_past.astype(F32), SUBLANES)
    o_b, s_new = _gdn(proj3, tail0, s_past.astype(F32), w['wc_gdn'], w['a_log'], w['dt_bias'], w['g_gdn_out'])
    x1 = _out_proj(o_a, o_b.reshape(m, GDN_OUT), w['w_out'], xf, m)

    x1t = x1.reshape(nb, lb, D_MODEL).transpose(1, 0, 2).reshape(m, D_MODEL)
    n_hist = FFN_CONV - 1
    halo0 = ffn_past.astype(F32).transpose(1, 0, 2).reshape(1, n_hist * nb, D_FF)
    yt, gate_tail = _ffn(x1t, w['g_ffn'], halo0, w['wg'], w['wu'], w['wc_ffn'], w['bc_ffn'], w['wd'],
                         m, 512, n_hist * nb, nb, 1)
    y = yt.reshape(lb, nb, D_MODEL).transpose(1, 0, 2)
    state = (c_kv.reshape(nb, lb, KV_LORA), k_rope.reshape(nb, lb, QK_ROPE),
             proj3[:, lb - (GDN_CONV - 1):, COL_QKV:COL_QKV + GDN_CONV_DIM], s_new,
             gate_tail.reshape(n_hist, nb, D_FF).transpose(1, 0, 2))
    return y, state


def kernel(x_prompt, x_sample, cache_mla_latent, cache_mla_krope, state_gdn_conv, state_gdn_S, state_ffn_conv,
           g_attn_norm, w_in, g_q_lat, g_kv_lat, w_q_up, w_kv_up, g_q_nope, g_q_rope, g_k_nope, g_k_rope,
           w_gdn_conv, a_log, dt_bias, g_gdn_out, w_out, g_ffn_norm, w_ffn_gate, w_ffn_up, w_ffn_conv,
           b_ffn_conv, w_ffn_down):
    xp, xs = x_prompt, x_sample
    new_p, new_s = [], []
    for l in range(w_in.shape[0]):
        w = _prep_weights(dict(
            g_attn_norm=g_attn_norm[l], w_in=w_in[l], g_q_lat=g_q_lat[l], g_kv_lat=g_kv_lat[l], w_q_up=w_q_up[l],
            w_kv_up=w_kv_up[l], g_q_nope=g_q_nope[l], g_q_rope=g_q_rope[l], g_k_nope=g_k_nope[l],
            g_k_rope=g_k_rope[l], w_gdn_conv=w_gdn_conv[l], a_log=a_log[l], dt_bias=dt_bias[l],
            g_gdn_out=g_gdn_out[l], w_out=w_out[l], g_ffn_norm=g_ffn_norm[l], w_ffn_gate=w_ffn_gate[l],
            w_ffn_up=w_ffn_up[l], w_ffn_conv=w_ffn_conv[l], b_ffn_conv=b_ffn_conv[l], w_ffn_down=w_ffn_down[l]))
        xp, st_p = _prompt_layer(xp, w)
        xs, st_s = _sample_layer(xs, cache_mla_latent[l], cache_mla_krope[l], state_gdn_conv[l], state_gdn_S[l],
                                 state_ffn_conv[l], w)
        new_p.append(st_p)
        new_s.append(st_s)
    p_state = [jnp.stack(t) for t in zip(*new_p)]
    s_state = [jnp.stack(t) for t in zip(*new_s)]
    return (xp, xs, *p_state, *s_state)
```

```python
import functools
import math

import jax
import jax.numpy as jnp
import numpy as np
from jax import lax
from jax.experimental import pallas as pl
from jax.experimental.pallas import tpu as pltpu

D_MODEL = 2048
CHUNK = 64
EPS = 1e-6
MLA_HEADS = 8
Q_LORA = 512
KV_LORA = 512
QK_NOPE = 128
QK_ROPE = 64
V_HEAD = 128
ROPE_THETA = 10000.0
GDN_HEADS = 8
GDN_DK = 128
GDN_DV = 128
GDN_CONV = 4
GDN_CONV_DIM = 2 * GDN_HEADS * GDN_DK + GDN_HEADS * GDN_DV
D_FF = 5632
FFN_CONV = 3
MLA_OUT = MLA_HEADS * V_HEAD
GDN_OUT = GDN_HEADS * GDN_DV
QK_DIM = QK_NOPE + QK_ROPE

LANES = 128
SUBLANES = 8
GDN_BLOCK = 128
GDN_BLOCKS_PER_STEP = 2

COL_QKV = 0
COL_Z = COL_QKV + GDN_CONV_DIM
COL_QA = COL_Z + GDN_OUT
COL_KVA = COL_QA + Q_LORA
COL_KR = COL_KVA + KV_LORA
COL_AB = COL_KR + LANES
D_IN_PAD = COL_AB + LANES

BF16 = jnp.bfloat16
F32 = jnp.float32
NT_DIMS = (((1,), (1,)), ((), ()))
TN_DIMS = (((0,), (0,)), ((), ()))


def _params(sem, vmem_mb):
    return pltpu.CompilerParams(dimension_semantics=sem, vmem_limit_bytes=vmem_mb * 1024 * 1024)


def _rms(x, g):
    return x * lax.rsqrt(jnp.mean(x * x, axis=-1, keepdims=True) + EPS) * g


def _sigmoid(x):
    return 1.0 / (1.0 + jnp.exp(-x))


def _softplus(x):
    return jnp.maximum(x, 0.0) + jnp.log(1.0 + jnp.exp(-jnp.abs(x)))


def _norm_matmul_kernel(x_ref, g_ref, w_ref, o_ref, h_ref):
    @pl.when(pl.program_id(1) == 0)
    def _():
        h_ref[...] = _rms(x_ref[...], g_ref[...]).astype(BF16)

    o_ref[...] = lax.dot_general(h_ref[...], w_ref[...], NT_DIMS, preferred_element_type=F32)


def _norm_matmul(x, g, wt, tm, tn):
    m, k = x.shape
    n = wt.shape[0]
    return pl.pallas_call(
        _norm_matmul_kernel,
        grid=(m // tm, n // tn),
        in_specs=[pl.BlockSpec((tm, k), lambda i, j: (i, 0)),
                  pl.BlockSpec((1, k), lambda i, j: (0, 0)),
                  pl.BlockSpec((tn, k), lambda i, j: (j, 0))],
        out_specs=pl.BlockSpec((tm, tn), lambda i, j: (i, j)),
        out_shape=jax.ShapeDtypeStruct((m, n), F32),
        scratch_shapes=[pltpu.VMEM((tm, k), BF16)],
        compiler_params=_params(("arbitrary", "arbitrary"), 56),
        name="in_proj",
    )(x, g, wt)


def _rope_pairs(y, cos, sin):
    lane = lax.broadcasted_iota(jnp.int32, (1, LANES), 1)
    first_half = (lane % QK_ROPE) < (QK_ROPE // 2)
    swapped = jnp.where(first_half, pltpu.roll(y, LANES - QK_ROPE // 2, 1), pltpu.roll(y, QK_ROPE // 2, 1))
    return y * cos + swapped * sin


def _mla_pre_kernel(qa_ref, kva_ref, kr_ref, cos_ref, sin_ref, gq_ref, gkv_ref, wq_ref, gqn_ref, gqr_ref,
                    gkr_ref, *rest, scale, with_kv):
    if with_kv:
        wk_ref, wv_ref, gk_ref, ckv_ref, krope_ref, qf_ref, kf_ref, v_ref = rest
    else:
        ckv_ref, krope_ref, qf_ref = rest
    cos = cos_ref[...]
    sin = sin_ref[...]
    lane = lax.broadcasted_iota(jnp.int32, (1, LANES), 1)
    lo = lane < QK_ROPE

    ckv = _rms(kva_ref[...], gkv_ref[...])
    ckv_ref[...] = ckv

    kr = kr_ref[...]
    ss = jnp.sum(jnp.where(lo, kr * kr, 0.0), axis=-1, keepdims=True)
    kr = kr * lax.rsqrt(ss * (1.0 / QK_ROPE) + EPS) * gkr_ref[...]
    krope = _rope_pairs(kr, cos, sin)[:, :QK_ROPE]
    krope_ref[...] = krope

    if with_kv:
        latb = ckv.astype(BF16)
        kn = jnp.dot(latb, wk_ref[...], preferred_element_type=F32)
        vv = jnp.dot(latb, wv_ref[...], preferred_element_type=F32)
        krb = krope.astype(BF16)
        gk = gk_ref[...]
        for h in range(MLA_HEADS):
            kf_ref[0, h, :, 0:QK_NOPE] = _rms(kn[:, h * QK_NOPE:(h + 1) * QK_NOPE], gk).astype(BF16)
            kf_ref[0, h, :, QK_NOPE:QK_DIM] = krb
            v_ref[0, h] = vv[:, h * V_HEAD:(h + 1) * V_HEAD].astype(BF16)

    hq = _rms(qa_ref[...], gq_ref[...]).astype(BF16)
    q = jnp.dot(hq, wq_ref[...], preferred_element_type=F32)
    gqn = gqn_ref[...] * scale
    for h in range(MLA_HEADS):
        xn = q[:, h * QK_NOPE:(h + 1) * QK_NOPE]
        qf_ref[0, h, :, 0:QK_NOPE] = (_rms(xn, gqn)).astype(BF16)
    gqr = gqr_ref[...] * scale
    rope0 = MLA_HEADS * QK_NOPE
    for p in range(MLA_HEADS // 2):
        xr = q[:, rope0 + p * LANES: rope0 + (p + 1) * LANES]
        sq = xr * xr
        s_lo = jnp.sum(jnp.where(lo, sq, 0.0), axis=-1, keepdims=True)
        s_hi = jnp.sum(jnp.where(lo, 0.0, sq), axis=-1, keepdims=True)
        r = jnp.where(lo, lax.rsqrt(s_lo * (1.0 / QK_ROPE) + EPS), lax.rsqrt(s_hi * (1.0 / QK_ROPE) + EPS))
        ro = _rope_pairs(xr * r * gqr, cos, sin).astype(BF16)
        qf_ref[0, 2 * p, :, QK_NOPE:QK_DIM] = ro[:, :QK_ROPE]
        qf_ref[0, 2 * p + 1, :, QK_NOPE:QK_DIM] = ro[:, QK_ROPE:]


def _mla_pre(proj, cos, sin, g_q_lat, g_kv_lat, wq, g_q_nope, g_q_rope2, g_k_rope2, nb, lb, tm, kv_weights=None):
    m = proj.shape[0]
    per_seq = lb // tm
    n_tab = cos.shape[0] // tm
    scale = float(QK_DIM) ** -0.5 * math.log2(math.e)
    row = lambda i: (i, 0)
    const = lambda i: (0, 0)
    heads = lambda i: (i // per_seq, 0, i % per_seq, 0)
    with_kv = kv_weights is not None
    in_specs = [pl.BlockSpec((tm, Q_LORA), lambda i: (i, COL_QA // Q_LORA)),
                pl.BlockSpec((tm, KV_LORA), lambda i: (i, COL_KVA // KV_LORA)),
                pl.BlockSpec((tm, LANES), lambda i: (i, COL_KR // LANES)),
                pl.BlockSpec((tm, LANES), lambda i: (i % n_tab, 0)),
                pl.BlockSpec((tm, LANES), lambda i: (i % n_tab, 0)),
                pl.BlockSpec((1, Q_LORA), const),
                pl.BlockSpec((1, KV_LORA), const),
                pl.BlockSpec(wq.shape, const),
                pl.BlockSpec((1, QK_NOPE), const),
                pl.BlockSpec((1, LANES), const),
                pl.BlockSpec((1, LANES), const)]
    out_specs = [pl.BlockSpec((tm, KV_LORA), row),
                 pl.BlockSpec((tm, QK_ROPE), row),
                 pl.BlockSpec((1, MLA_HEADS, tm, QK_DIM), heads)]
    out_shape = [jax.ShapeDtypeStruct((m, KV_LORA), F32),
                 jax.ShapeDtypeStruct((m, QK_ROPE), F32),
                 jax.ShapeDtypeStruct((nb, MLA_HEADS, lb, QK_DIM), BF16)]
    operands = [proj, proj, proj, cos, sin, g_q_lat, g_kv_lat, wq, g_q_nope, g_q_rope2, g_k_rope2]
    if with_kv:
        wk, wv, g_k_nope = kv_weights
        in_specs += [pl.BlockSpec(wk.shape, const), pl.BlockSpec(wv.shape, const),
                     pl.BlockSpec((1, QK_NOPE), const)]
        out_specs += [pl.BlockSpec((1, MLA_HEADS, tm, QK_DIM), heads),
                      pl.BlockSpec((1, MLA_HEADS, tm, V_HEAD), heads)]
        out_shape += [jax.ShapeDtypeStruct((nb, MLA_HEADS, lb, QK_DIM), BF16),
                      jax.ShapeDtypeStruct((nb, MLA_HEADS, lb, V_HEAD), BF16)]
        operands += [wk, wv, g_k_nope]
    return pl.pallas_call(
        functools.partial(_mla_pre_kernel, scale=scale, with_kv=with_kv),
        grid=(m // tm,),
        in_specs=in_specs,
        out_specs=out_specs,
        out_shape=out_shape,
        compiler_params=_params(("arbitrary",), 40),
        name="mla_pre",
    )(*operands)


def _flash_kernel(q_ref, k_ref, v_ref, o_ref, m_ref, l_ref, acc_ref, *, tq, tk, sub):
    qi = pl.program_id(2)
    m_ref[...] = jnp.full(m_ref.shape, -jnp.inf, F32)
    l_ref[...] = jnp.zeros(l_ref.shape, F32)
    acc_ref[...] = jnp.zeros(acc_ref.shape, F32)

    def attend_rows(r0, nr, keys, k0, masked):
        rows = pl.ds(r0, nr)
        s = lax.dot_general(q_ref[0, 0, rows, :], k_ref[0, 0, keys, :], NT_DIMS, preferred_element_type=F32)
        if masked:
            rc = (r0 + lax.broadcasted_iota(jnp.int32, (nr, tk), 0)) // CHUNK
            cc = (k0 + lax.broadcasted_iota(jnp.int32, (nr, tk), 1)) // CHUNK
            s = jnp.where(cc <= rc, s, -jnp.inf)
        m = m_ref[rows, :]
        m_new = jnp.maximum(m, jnp.max(s, axis=-1, keepdims=True))
        alpha = jnp.exp2(m - m_new)
        pc = [jnp.exp2(s[:, c * LANES:(c + 1) * LANES] - m_new) for c in range(tk // LANES)]
        psum = pc[0]
        for c in range(1, tk // LANES):
            psum = psum + pc[c]
        p = jnp.concatenate(pc, axis=1)
        m_ref[rows, :] = m_new
        l_ref[rows, :] = alpha * l_ref[rows, :] + psum
        acc_ref[rows, :] = alpha * acc_ref[rows, :] + jnp.dot(p.astype(BF16), v_ref[0, 0, keys, :],
                                                              preferred_element_type=F32)

    def attend(r_lo, r_hi, ki, k0, masked):
        keys = pl.ds(pl.multiple_of(ki * tk, tk), tk)
        for r0 in range(r_lo, r_hi, sub):
            if not masked or r0 + sub > k0:
                attend_rows(r0, sub, keys, k0, masked and r0 < k0 + tk)

    def body(ki, c):
        attend(0, tq, ki, 0, False)
        return c

    n_diag = tq // tk
    lax.fori_loop(0, n_diag * qi, body, 0)
    for j in range(n_diag):
        attend(0, tq, n_diag * qi + j, j * tk, True)
    l = jnp.sum(l_ref[...], axis=-1, keepdims=True)
    o_ref[0] = (acc_ref[...] / l).astype(BF16)


def _flash_attention(qf, kf, v, tq, tk, sub):
    nb, nh, lq, _ = qf.shape
    t = kf.shape[2]
    return pl.pallas_call(
        functools.partial(_flash_kernel, tq=tq, tk=tk, sub=sub),
        grid=(nb, nh, lq // tq),
        in_specs=[pl.BlockSpec((1, 1, tq, QK_DIM), lambda b, h, i: (b, h, i, 0)),
                  pl.BlockSpec((1, 1, t, QK_DIM), lambda b, h, i: (b, h, 0, 0)),
                  pl.BlockSpec((1, 1, t, V_HEAD), lambda b, h, i: (b, h, 0, 0))],
        out_specs=pl.BlockSpec((1, tq, V_HEAD), lambda b, h, i: (b, i, h)),
        out_shape=jax.ShapeDtypeStruct((nb, lq, nh * V_HEAD), BF16),
        scratch_shapes=[pltpu.VMEM((tq, LANES), F32), pltpu.VMEM((tq, LANES), F32),
                        pltpu.VMEM((tq, V_HEAD), F32)],
        compiler_params=_params(("arbitrary", "arbitrary", "arbitrary"), 40),
        name="flash_attn",
    )(qf, kf, v)


def _attn_cached_kernel(qn_ref, qr_ref, lat_ref, kr_ref, latn_ref, krn_ref, wk_ref, wv_ref, gk_ref, o_ref,
                        m_ref, l_ref, acc_ref, *, lq):
    kt = pl.program_id(1)

    @pl.when(kt == 0)
    def _():
        m_ref[...] = jnp.full(m_ref.shape, -jnp.inf, F32)
        l_ref[...] = jnp.zeros(l_ref.shape, F32)
        acc_ref[...] = jnp.zeros(acc_ref.shape, F32)

    def attend(lat, kr_t):
        latb = lat.astype(BF16)
        kn = jnp.dot(latb, wk_ref[...], preferred_element_type=F32)
        gk = gk_ref[...]
        knb = jnp.concatenate([_rms(kn[:, h * QK_NOPE:(h + 1) * QK_NOPE], gk).astype(BF16)
                               for h in range(MLA_HEADS)], axis=1)
        s = (lax.dot_general(qn_ref[0], knb, NT_DIMS, preferred_element_type=F32)
             + jnp.dot(qr_ref[0], kr_t.astype(BF16), preferred_element_type=F32))
        m = m_ref[...]
        m_new = jnp.maximum(m, jnp.max(s, axis=-1, keepdims=True))
        alpha = jnp.exp2(m - m_new)
        p = jnp.exp2(s - m_new[:, 0:1])
        m_ref[...] = m_new
        l_ref[...] = alpha * l_ref[...] + jnp.sum(p, axis=-1, keepdims=True)
        acc_ref[...] = alpha[:, 0:1] * acc_ref[...] + jnp.dot(p.astype(BF16), latb, preferred_element_type=F32)

    attend(lat_ref[0], kr_ref[0])

    @pl.when(kt == pl.num_programs(1) - 1)
    def _():
        attend(latn_ref[...], krn_ref[0])
        o_lat = (acc_ref[...] / l_ref[:, 0:1]).astype(BF16)
        for h in range(MLA_HEADS):
            o_ref[:, h * V_HEAD:(h + 1) * V_HEAD] = jnp.dot(
                o_lat[h * lq:(h + 1) * lq, :], wv_ref[:, h * V_HEAD:(h + 1) * V_HEAD],
                preferred_element_type=F32).astype(BF16)


def _attn_cached(qn_bd, qr, lat_past, kr_past, lat_new, kr_new, wk, wv, g_k_nope, lq, tk):
    nb, past, _ = lat_past.shape
    nrow = MLA_HEADS * lq
    kr_past_t = kr_past.transpose(0, 2, 1)
    kr_new_t = kr_new.reshape(nb, lq, QK_ROPE).transpose(0, 2, 1)
    per_b = lambda b, k: (b, 0, 0)
    tile = lambda b, k: (b, k, 0)
    new = lambda b, k: (b, 0)
    const = lambda b, k: (0, 0)
    return pl.pallas_call(
        functools.partial(_attn_cached_kernel, lq=lq),
        grid=(nb, past // tk),
        in_specs=[pl.BlockSpec((1, nrow, MLA_HEADS * QK_NOPE), per_b),
                  pl.BlockSpec((1, nrow, QK_ROPE), per_b),
                  pl.BlockSpec((1, tk, KV_LORA), tile),
                  pl.BlockSpec((1, QK_ROPE, tk), lambda b, k: (b, 0, k)),
                  pl.BlockSpec((lq, KV_LORA), new),
                  pl.BlockSpec((1, QK_ROPE, lq), per_b),
                  pl.BlockSpec(wk.shape, const),
                  pl.BlockSpec(wv.shape, const),
                  pl.BlockSpec((1, QK_NOPE), const)],
        out_specs=pl.BlockSpec((lq, MLA_HEADS * V_HEAD), new),
        out_shape=jax.ShapeDtypeStruct((nb * lq, MLA_HEADS * V_HEAD), BF16),
        scratch_shapes=[pltpu.VMEM((nrow, LANES), F32), pltpu.VMEM((nrow, LANES), F32),
                        pltpu.VMEM((nrow, KV_LORA), F32)],
        compiler_params=_params(("arbitrary", "arbitrary"), 40),
        name="attn_cached",
    )(qn_bd, qr, lat_past, kr_past_t, lat_new, kr_new_t, wk, wv, g_k_nope)


def _gdn_kernel(qkv_ref, z_ref, abc_ref, abr_ref, tail0_ref, s0_ref, wc_ref, alog_c_ref, dt_c_ref, alog_r_ref,
                dt_r_ref, gout_ref, o_ref, sfin_ref, ext_ref, s_ref, *, blk, nsub, valid, n_levels):
    t = pl.program_id(1)
    nt = pl.num_programs(1)
    halo = SUBLANES
    rows = nsub * blk
    n_real = rows if valid == blk else valid

    @pl.when(t == 0)
    def _():
        ext_ref[0:halo, :] = tail0_ref[0]
        s_ref[...] = s0_ref[0]

    @pl.when(t > 0)
    def _():
        ext_ref[0:halo, :] = ext_ref[rows:rows + halo, :]

    def pad_rows(x, to=rows):
        return x if x.shape[0] == to else jnp.concatenate(
            [x, jnp.zeros((to - x.shape[0], x.shape[1]), x.dtype)], axis=0)

    conv_rows = -(-n_real // SUBLANES) * SUBLANES
    ext_ref[halo:halo + conv_rows, :] = pad_rows(qkv_ref[0], conv_rows)
    wc = wc_ref[...]
    conv = wc[GDN_CONV - 1:GDN_CONV, :] * ext_ref[halo:halo + conv_rows, :]
    for i in range(1, GDN_CONV):
        conv = conv + wc[GDN_CONV - 1 - i:GDN_CONV - i, :] * ext_ref[halo - i:halo - i + conv_rows, :]
    act = pad_rows(conv * _sigmoid(conv))

    abc = pad_rows(abc_ref[0])
    abr = abr_ref[0, 0]
    rvalid = lax.broadcasted_iota(jnp.int32, (rows, 1), 0) < n_real
    cvalid = lax.broadcasted_iota(jnp.int32, (1, rows), 1) < n_real
    g_col = jnp.where(rvalid, -jnp.exp(alog_c_ref[...]) * _softplus(abc + dt_c_ref[...]), 0.0)
    beta_col = jnp.where(rvalid, _sigmoid(abc), 0.0)
    g_row = jnp.where(cvalid, -jnp.exp(alog_r_ref[...]) * _softplus(abr + dt_r_ref[...]), 0.0)
    ii = lax.broadcasted_iota(jnp.int32, (blk, blk), 0)
    jj = lax.broadcasted_iota(jnp.int32, (blk, blk), 1)
    incl = ii >= jj
    lower = incl.astype(F32)
    upper = (ii <= jj).astype(F32)
    gc_cols = [jnp.dot(lower, g_col[sb * blk:(sb + 1) * blk], preferred_element_type=F32,
                       precision=lax.Precision.HIGHEST) for sb in range(nsub)]
    gc_rows = [jnp.dot(g_row[:, sb * blk:(sb + 1) * blk], upper, preferred_element_type=F32,
                       precision=lax.Precision.HIGHEST) for sb in range(nsub)]
    eye = (ii == jj).astype(F32)
    merge_masks = []
    for lvl in range(n_levels):
        half = 1 << lvl
        merge_masks.append((ii // (2 * half) == jj // (2 * half)) & ((ii // half) % 2 == 1)
                           & ((jj // half) % 2 == 0))
    nk = GDN_HEADS * GDN_DK
    gout = gout_ref[...]

    heads = range(GDN_HEADS)
    units = [(sb, h) for sb in range(nsub) for h in heads]
    s_cur = [s_ref[h] for h in heads]
    n_out = blk if n_real == rows else n_real
    zs = {(sb, h): z_ref[0, sb * blk:sb * blk + n_out, h * GDN_DV:(h + 1) * GDN_DV] for sb, h in units}
    q, k, v, gc, beta, decay = {}, {}, {}, {}, {}, {}
    for sb, h in units:
        rs = slice(sb * blk, (sb + 1) * blk)
        qh = act[rs, h * GDN_DK:(h + 1) * GDN_DK]
        kh = act[rs, nk + h * GDN_DK: nk + (h + 1) * GDN_DK]
        u = (sb, h)
        q[u] = qh * lax.rsqrt(jnp.sum(qh * qh, axis=-1, keepdims=True) + EPS) * (float(GDN_DK) ** -0.5)
        k[u] = kh * lax.rsqrt(jnp.sum(kh * kh, axis=-1, keepdims=True) + EPS)
        v[u] = act[rs, 2 * nk + h * GDN_DV: 2 * nk + (h + 1) * GDN_DV]
        gc[u] = gc_cols[sb][:, h:h + 1]
        beta[u] = beta_col[rs, GDN_HEADS + h:GDN_HEADS + h + 1]
        decay[u] = jnp.where(incl, jnp.exp(gc[u] - gc_rows[sb][h:h + 1, :]), 0.0)
    kb = {u: k[u].astype(BF16) for u in units}
    kq = {u: lax.dot_general(jnp.concatenate([kb[u], q[u].astype(BF16)], axis=0), kb[u], NT_DIMS,
                             preferred_element_type=F32) for u in units}
    a = {u: beta[u] * kq[u][0:blk] * decay[u] for u in units}
    tinv = {u: eye - jnp.where(merge_masks[0], a[u], 0.0) for u in units}
    for lvl in range(1, n_levels):
        tb = {u: tinv[u].astype(BF16) for u in units}
        y = {u: jnp.dot(jnp.where(merge_masks[lvl], a[u], 0.0).astype(BF16), tb[u], preferred_element_type=F32)
             for u in units}
        tinv = {u: tinv[u] - jnp.dot(tb[u], y[u].astype(BF16), preferred_element_type=F32) for u in units}
    egc_all = [jnp.exp(gc_cols[sb]) for sb in range(nsub)]
    last_all = [gc_cols[sb][blk - 1:blk, :] for sb in range(nsub)]
    kdec_all = [jnp.exp(last_all[sb] - gc_cols[sb]) for sb in range(nsub)]
    slast_all = [jnp.exp(last_all[sb]) for sb in range(nsub)]
    egc = {(sb, h): egc_all[sb][:, h:h + 1] for sb, h in units}
    uw = {u: jnp.dot(tinv[u].astype(BF16),
                     jnp.concatenate([v[u] * beta[u], k[u] * (beta[u] * egc[u])], axis=1).astype(BF16),
                     preferred_element_type=F32) for u in units}
    o = {}
    for sb in range(nsub):
        sbf = [x.astype(BF16) for x in s_cur]
        ws = [jnp.dot(jnp.concatenate([uw[sb, h][:, GDN_DV:], q[sb, h] * egc[sb, h]], axis=0).astype(BF16),
                      sbf[h], preferred_element_type=F32) for h in heads]
        vb = [(uw[sb, h][:, 0:GDN_DV] - ws[h][0:blk]).astype(BF16) for h in heads]
        for h in heads:
            o[sb, h] = ws[h][blk:2 * blk] + jnp.dot((kq[sb, h][blk:2 * blk] * decay[sb, h]).astype(BF16), vb[h],
                                                    preferred_element_type=F32)
        s_cur = [s_cur[h] * slast_all[sb][:, h:h + 1]
                 + lax.dot_general((k[sb, h] * kdec_all[sb][:, h:h + 1]).astype(BF16), vb[h], TN_DIMS,
                                   preferred_element_type=F32) for h in heads]
    for h in heads:
        s_ref[h] = s_cur[h]
    for sb, h in units:
        o_ref[0, sb * blk:sb * blk + n_out, h * GDN_DV:(h + 1) * GDN_DV] = (
            _rms(o[sb, h][0:n_out], gout) * (zs[sb, h] * _sigmoid(zs[sb, h]))).astype(BF16)

    @pl.when(t == nt - 1)
    def _():
        sfin_ref[0] = s_ref[...]


def _gdn(proj3, tail0, s0, wc, a_log, dt_bias, g_out):
    nb, t, _ = proj3.shape
    blk = GDN_BLOCK
    valid = min(t, blk)
    nsub = GDN_BLOCKS_PER_STEP if t % (GDN_BLOCKS_PER_STEP * blk) == 0 else 1
    rin = nsub * valid
    rows = nsub * blk
    ab = proj3[:, :, COL_AB:COL_AB + 2 * GDN_HEADS].reshape(nb, t // rin, rin, 2 * GDN_HEADS)
    ab_rows = jnp.pad(ab.transpose(0, 1, 3, 2), ((0, 0), (0, 0), (0, 0), (0, rows - rin)))
    pad_c = lambda v: jnp.zeros((1, LANES), F32).at[0, :GDN_HEADS].set(v)
    pad_r = lambda v: jnp.zeros((2 * GDN_HEADS, 1), F32).at[:GDN_HEADS, 0].set(v)
    const2 = lambda b, i: (0, 0)
    return pl.pallas_call(
        functools.partial(_gdn_kernel, blk=blk, nsub=nsub, valid=valid, n_levels=int(math.log2(blk))),
        grid=(nb, t // rin),
        in_specs=[pl.BlockSpec((1, rin, GDN_CONV_DIM), lambda b, i: (b, i, COL_QKV // GDN_CONV_DIM)),
                  pl.BlockSpec((1, rin, GDN_OUT), lambda b, i: (b, i, COL_Z // GDN_OUT)),
                  pl.BlockSpec((1, rin, LANES), lambda b, i: (b, i, COL_AB // LANES)),
                  pl.BlockSpec((1, 1, 2 * GDN_HEADS, rows), lambda b, i: (b, i, 0, 0)),
                  pl.BlockSpec((1, SUBLANES, GDN_CONV_DIM), lambda b, i: (b, 0, 0)),
                  pl.BlockSpec((1, GDN_HEADS, GDN_DK, GDN_DV), lambda b, i: (b, 0, 0, 0)),
                  pl.BlockSpec((GDN_CONV, GDN_CONV_DIM), const2),
                  pl.BlockSpec((1, LANES), const2),
                  pl.BlockSpec((1, LANES), const2),
                  pl.BlockSpec((2 * GDN_HEADS, 1), const2),
                  pl.BlockSpec((2 * GDN_HEADS, 1), const2),
                  pl.BlockSpec((1, GDN_DV), const2)],
        out_specs=[pl.BlockSpec((1, rin, GDN_OUT), lambda b, i: (b, i, 0)),
                   pl.BlockSpec((1, GDN_HEADS, GDN_DK, GDN_DV), lambda b, i: (b, 0, 0, 0))],
        out_shape=[jax.ShapeDtypeStruct((nb, t, GDN_OUT), BF16),
                   jax.ShapeDtypeStruct((nb, GDN_HEADS, GDN_DK, GDN_DV), F32)],
        scratch_shapes=[pltpu.VMEM((rows + 2 * SUBLANES, GDN_CONV_DIM), F32),
                        pltpu.VMEM((GDN_HEADS, GDN_DK, GDN_DV), F32)],
        compiler_params=_params(("arbitrary", "arbitrary"), 40),
        name="gdn",
    )(proj3, proj3, proj3, ab_rows, tail0, s0, wc, pad_c(a_log), pad_c(dt_bias), pad_r(a_log), pad_r(dt_bias),
      g_out)


def _out_proj_kernel(oa_ref, ob_ref, w_ref, x_ref, y_ref):
    y_ref[...] = (x_ref[...]
                  + jnp.dot(oa_ref[...], w_ref[0:MLA_OUT, :], preferred_element_type=F32)
                  + jnp.dot(ob_ref[...], w_ref[MLA_OUT:MLA_OUT + GDN_OUT, :], preferred_element_type=F32))


def _out_proj(o_a, o_b, w, x, tm):
    m = x.shape[0]
    row = lambda i: (i, 0)
    return pl.pallas_call(
        _out_proj_kernel,
        grid=(m // tm,),
        in_specs=[pl.BlockSpec((tm, MLA_OUT), row),
                  pl.BlockSpec((tm, GDN_OUT), row),
                  pl.BlockSpec(w.shape, lambda i: (0, 0)),
                  pl.BlockSpec((tm, D_MODEL), row)],
        out_specs=pl.BlockSpec((tm, D_MODEL), row),
        out_shape=jax.ShapeDtypeStruct((m, D_MODEL), F32),
        compiler_params=_params(("arbitrary",), 48),
        name="out_proj",
    )(o_a, o_b, w, x)


def _ffn_kernel(x_ref, g_ref, halo0_ref, wg_ref, wu_ref, wc_ref, bc_ref, wd_ref, y_ref, st_ref, h_ref, ext_ref,
                carry_ref, *, tm, rows, halo, shift, per_seq):
    i = pl.program_id(0)
    f = pl.program_id(1)

    @pl.when(f == 0)
    def _():
        x = x_ref[...]
        h_ref[...] = _rms(x, g_ref[...]).astype(BF16)
        y_ref[...] = x

    @pl.when(i % per_seq == 0)
    def _():
        ext_ref[0:halo, :] = halo0_ref[0]

    @pl.when(i % per_seq != 0)
    def _():
        ext_ref[0:halo, :] = carry_ref[f]

    wc = wc_ref[...]
    bc = bc_ref[...]
    for r0 in range(0, tm, rows):
        h = h_ref[r0:r0 + rows, :]
        gate = jnp.dot(h, wg_ref[...], preferred_element_type=F32)
        up = jnp.dot(h, wu_ref[...], preferred_element_type=F32)
        ext_ref[halo + r0:halo + r0 + rows, :] = gate
        gc = (wc[2:3, :] * gate + wc[1:2, :] * ext_ref[halo + r0 - shift:halo + r0 - shift + rows, :]
              + wc[0:1, :] * ext_ref[halo + r0 - 2 * shift:halo + r0 - 2 * shift + rows, :] + bc)
        act = (gc * _sigmoid(gc)) * up
        y_ref[r0:r0 + rows, :] += jnp.dot(act.astype(BF16), wd_ref[...], preferred_element_type=F32)
    last = ext_ref[tm:tm + halo, :]
    carry_ref[f] = last
    st_ref[0] = last


def _ffn(x, g, halo0, wg, wu, wc, bc, wd, tm, tf, halo, shift, per_seq):
    m = x.shape[0]
    nf = D_FF // tf
    return pl.pallas_call(
        functools.partial(_ffn_kernel, tm=tm, rows=min(tm, 512), halo=halo, shift=shift, per_seq=per_seq),
        grid=(m // tm, nf),
        in_specs=[pl.BlockSpec((tm, D_MODEL), lambda i, f: (i, 0), pipeline_mode=pl.Buffered(1)),
                  pl.BlockSpec((1, D_MODEL), lambda i, f: (0, 0)),
                  pl.BlockSpec((1, halo, tf), lambda i, f: (i // per_seq, 0, f)),
                  pl.BlockSpec((D_MODEL, tf), lambda i, f: (0, f)),
                  pl.BlockSpec((D_MODEL, tf), lambda i, f: (0, f)),
                  pl.BlockSpec((FFN_CONV, tf), lambda i, f: (0, f)),
                  pl.BlockSpec((1, tf), lambda i, f: (0, f)),
                  pl.BlockSpec((tf, D_MODEL), lambda i, f: (f, 0))],
        out_specs=[pl.BlockSpec((tm, D_MODEL), lambda i, f: (i, 0)),
                   pl.BlockSpec((1, halo, tf), lambda i, f: (i, 0, f))],
        out_shape=[jax.ShapeDtypeStruct((m, D_MODEL), F32),
                   jax.ShapeDtypeStruct((m // tm, halo, D_FF), F32)],
        scratch_shapes=[pltpu.VMEM((tm, D_MODEL), BF16),
                        pltpu.VMEM((halo + tm, tf), F32),
                        pltpu.VMEM((nf, halo, tf), F32)],
        compiler_params=_params(("arbitrary", "arbitrary"), 60),
        name="conv_ffn",
    )(x, g, halo0, wg, wu, wc, bc, wd)


def _rope_tables(pos, reps):
    half = QK_ROPE // 2
    inv = 1.0 / (ROPE_THETA ** (jnp.arange(half, dtype=F32) / half))
    ang = pos.astype(F32)[:, None] * inv[None, :]
    cos, sin = jnp.cos(ang), jnp.sin(ang)
    cos = jnp.tile(jnp.concatenate([cos, cos], axis=-1), (reps, LANES // QK_ROPE))
    sin = jnp.tile(jnp.concatenate([-sin, sin], axis=-1), (reps, LANES // QK_ROPE))
    return cos, sin


def _prep_weights(lw):
    w_in = lw['w_in']
    off = np.cumsum([Q_LORA, KV_LORA, QK_ROPE, GDN_CONV_DIM, GDN_OUT, GDN_HEADS, GDN_HEADS]).tolist()
    wt = w_in.T
    zr = lambda n: jnp.zeros((n, D_MODEL), w_in.dtype)
    w_in_r = jnp.concatenate([wt[off[2]:off[4]], wt[:off[1]], wt[off[1]:off[2]], zr(LANES - QK_ROPE),
                              wt[off[4]:off[6]], zr(LANES - 2 * GDN_HEADS)], axis=0)
    wq = lw['w_q_up'].reshape(Q_LORA, MLA_HEADS, QK_DIM)
    wq_r = jnp.concatenate([wq[:, :, :QK_NOPE].reshape(Q_LORA, -1), wq[:, :, QK_NOPE:].reshape(Q_LORA, -1)], axis=1)
    wkv = lw['w_kv_up']
    row = lambda v: v.reshape(1, -1).astype(F32)
    return dict(
        w_in=w_in_r.astype(BF16), wq=wq_r.astype(BF16),
        wk=wkv[:, :, :QK_NOPE].reshape(KV_LORA, -1).astype(BF16),
        wv=wkv[:, :, QK_NOPE:].reshape(KV_LORA, -1).astype(BF16),
        w_out=lw['w_out'].astype(BF16), wg=lw['w_ffn_gate'].astype(BF16), wu=lw['w_ffn_up'].astype(BF16),
        wd=lw['w_ffn_down'].astype(BF16),
        g_attn=row(lw['g_attn_norm']), g_q_lat=row(lw['g_q_lat']), g_kv_lat=row(lw['g_kv_lat']),
        g_q_nope=row(lw['g_q_nope']), g_k_nope=row(lw['g_k_nope']),
        g_q_rope2=row(jnp.tile(lw['g_q_rope'], LANES // QK_ROPE)),
        g_k_rope2=row(jnp.tile(lw['g_k_rope'], LANES // QK_ROPE)),
        wc_gdn=lw['w_gdn_conv'].astype(F32), a_log=lw['a_log'].astype(F32), dt_bias=lw['dt_bias'].astype(F32),
        g_gdn_out=row(lw['g_gdn_out']), g_ffn=row(lw['g_ffn_norm']), wc_ffn=lw['w_ffn_conv'].astype(F32),
        bc_ffn=row(lw['b_ffn_conv']))


def _pad_rows_front(a, rows):
    return jnp.pad(a, ((0, 0), (rows - a.shape[1], 0), (0, 0)))


def _prompt_layer(x, w):
    nb, lb, _ = x.shape
    m = nb * lb
    xf = x.reshape(m, D_MODEL)
    proj = _norm_matmul(xf, w['g_attn'], w['w_in'], 1024, 1792)
    cos, sin = _rope_tables(jnp.arange(lb), 1)
    c_kv, k_rope, qf, kf, v = _mla_pre(proj, cos, sin, w['g_q_lat'], w['g_kv_lat'], w['wq'], w['g_q_nope'],
                                       w['g_q_rope2'], w['g_k_rope2'], nb, lb, 512,
                                       kv_weights=(w['wk'], w['wv'], w['g_k_nope']))
    o_a = _flash_attention(qf, kf, v, 4096, 1024, 1024).reshape(m, MLA_OUT)

    proj3 = proj.reshape(nb, lb, D_IN_PAD)
    tail0 = jnp.zeros((nb, SUBLANES, GDN_CONV_DIM), F32)
    s0 = jnp.zeros((nb, GDN_HEADS, GDN_DK, GDN_DV), F32)
    o_b, s_new = _gdn(proj3, tail0, s0, w['wc_gdn'], w['a_log'], w['dt_bias'], w['g_gdn_out'])
    x1 = _out_proj(o_a, o_b.reshape(m, GDN_OUT), w['w_out'], xf, 512)

    tm = 1024
    halo0 = jnp.zeros((nb, SUBLANES, D_FF), F32)
    y, gate_tail = _ffn(x1, w['g_ffn'], halo0, w['wg'], w['wu'], w['wc_ffn'], w['bc_ffn'], w['wd'],
                        tm, 512, SUBLANES, 1, lb // tm)
    state = (c_kv.reshape(nb, lb, KV_LORA), k_rope.reshape(nb, lb, QK_ROPE),
             proj3[:, lb - (GDN_CONV - 1):, COL_QKV:COL_QKV + GDN_CONV_DIM], s_new,
             gate_tail.reshape(nb, lb // tm, SUBLANES, D_FF)[:, -1, SUBLANES - (FFN_CONV - 1):, :])
    return y.reshape(nb, lb, D_MODEL), state


def _sample_layer(x, lat_past, krope_past, conv_past, s_past, ffn_past, w):
    nb, lb, _ = x.shape
    past = lat_past.shape[1]
    assert (past + lb - 1) // CHUNK == past // CHUNK and past % CHUNK == 0, "new frames must share one chunk"
    m = nb * lb
    xf = x.reshape(m, D_MODEL)
    proj = _norm_matmul(xf, w['g_attn'], w['w_in'], m, 768)
    cos, sin = _rope_tables(past + jnp.arange(lb), nb)
    c_kv, k_rope, qf = _mla_pre(proj, cos, sin, w['g_q_lat'], w['g_kv_lat'], w['wq'], w['g_q_nope'],
                                w['g_q_rope2'], w['g_k_rope2'], 1, m, m)
    qh = qf[0].reshape(MLA_HEADS, nb, lb, QK_DIM).transpose(1, 0, 2, 3)
    qn_bd = jnp.einsum('bhqd,hg->bhqgd', qh[..., :QK_NOPE], jnp.eye(MLA_HEADS, dtype=qh.dtype))
    qn_bd = qn_bd.reshape(nb, MLA_HEADS * lb, MLA_HEADS * QK_NOPE)
    qr = qh[..., QK_NOPE:].reshape(nb, MLA_HEADS * lb, QK_ROPE)
    o_a = _attn_cached(qn_bd, qr, lat_past, krope_past, c_kv, k_rope, w['wk'], w['wv'], w['g_k_nope'], lb, 1024)

    proj3 = proj.reshape(nb, lb, D_IN_PAD)
    tail0 = _pad_rows_front(conv_past.astype(F32), SUBLANES)
    o_b, s_new = _gdn(proj3, tail0, s_past.astype(F32), w['wc_gdn'], w['a_log'], w['dt_bias'], w['g_gdn_out'])
    x1 = _out_proj(o_a, o_b.reshape(m, GDN_OUT), w['w_out'], xf, m)

    x1t = x1.reshape(nb, lb, D_MODEL).transpose(1, 0, 2).reshape(m, D_MODEL)
    n_hist = FFN_CONV - 1
    halo0 = ffn_past.astype(F32).transpose(1, 0, 2).reshape(1, n_hist * nb, D_FF)
    yt, gate_tail = _ffn(x1t, w['g_ffn'], halo0, w['wg'], w['wu'], w['wc_ffn'], w['bc_ffn'], w['wd'],
                         m, 512, n_hist * nb, nb, 1)
    y = yt.reshape(lb, nb, D_MODEL).transpose(1, 0, 2)
    state = (c_kv.reshape(nb, lb, KV_LORA), k_rope.reshape(nb, lb, QK_ROPE),
             proj3[:, lb - (GDN_CONV - 1):, COL_QKV:COL_QKV + GDN_CONV_DIM], s_new,
             gate_tail.reshape(n_hist, nb, D_FF).transpose(1, 0, 2))
    return y, state


def kernel(x_prompt, x_sample, cache_mla_latent, cache_mla_krope, state_gdn_conv, state_gdn_S, state_ffn_conv,
           g_attn_norm, w_in, g_q_lat, g_kv_lat, w_q_up, w_kv_up, g_q_nope, g_q_rope, g_k_nope, g_k_rope,
           w_gdn_conv, a_log, dt_bias, g_gdn_out, w_out, g_ffn_norm, w_ffn_gate, w_ffn_up, w_ffn_conv,
           b_ffn_conv, w_ffn_down):
    xp, xs = x_prompt, x_sample
    new_p, new_s = [], []
    for l in range(w_in.shape[0]):
        w = _prep_weights(dict(
            g_attn_norm=g_attn_norm[l], w_in=w_in[l], g_q_lat=g_q_lat[l], g_kv_lat=g_kv_lat[l], w_q_up=w_q_up[l],
            w_kv_up=w_kv_up[l], g_q_nope=g_q_nope[l], g_q_rope=g_q_rope[l], g_k_nope=g_k_nope[l],
            g_k_rope=g_k_rope[l], w_gdn_conv=w_gdn_conv[l], a_log=a_log[l], dt_bias=dt_bias[l],
            g_gdn_out=g_gdn_out[l], w_out=w_out[l], g_ffn_norm=g_ffn_norm[l], w_ffn_gate=w_ffn_gate[l],
            w_ffn_up=w_ffn_up[l], w_ffn_conv=w_ffn_conv[l], b_ffn_conv=b_ffn_conv[l], w_ffn_down=w_ffn_down[l]))
        xp, st_p = _prompt_layer(xp, w)
        xs, st_s = _sample_layer(xs, cache_mla_latent[l], cache_mla_krope[l], state_gdn_conv[l], state_gdn_S[l],
                                 state_ffn_conv[l], w)
        new_p.append(st_p)
        new_s.append(st_s)
    p_state = [jnp.stack(t) for t in zip(*new_p)]
    s_state = [jnp.stack(t) for t in zip(*new_s)]
    return (xp, xs, *p_state, *s_state)
```

```python
import functools
import math

import jax
import jax.numpy as jnp
import numpy as np
from jax import lax
from jax.experimental import pallas as pl
from jax.experimental.pallas import tpu as pltpu

D_MODEL = 2048
CHUNK = 64
EPS = 1e-6
MLA_HEADS = 8
Q_LORA = 512
KV_LORA = 512
QK_NOPE = 128
QK_ROPE = 64
V_HEAD = 128
ROPE_THETA = 10000.0
GDN_HEADS = 8
GDN_DK = 128
GDN_DV = 128
GDN_CONV = 4
GDN_CONV_DIM = 2 * GDN_HEADS * GDN_DK + GDN_HEADS * GDN_DV
D_FF = 5632
FFN_CONV = 3
MLA_OUT = MLA_HEADS * V_HEAD
GDN_OUT = GDN_HEADS * GDN_DV
QK_DIM = QK_NOPE + QK_ROPE

LANES = 128
SUBLANES = 8
GDN_BLOCK = 128
GDN_BLOCKS_PER_STEP = 2

COL_QKV = 0
COL_Z = COL_QKV + GDN_CONV_DIM
COL_QA = COL_Z + GDN_OUT
COL_KVA = COL_QA + Q_LORA
COL_KR = COL_KVA + KV_LORA
COL_AB = COL_KR + LANES
D_IN_PAD = COL_AB + LANES

BF16 = jnp.bfloat16
F32 = jnp.float32
NT_DIMS = (((1,), (1,)), ((), ()))
TN_DIMS = (((0,), (0,)), ((), ()))


def _params(sem, vmem_mb):
    return pltpu.CompilerParams(dimension_semantics=sem, vmem_limit_bytes=vmem_mb * 1024 * 1024)


def _rms(x, g):
    return x * lax.rsqrt(jnp.mean(x * x, axis=-1, keepdims=True) + EPS) * g


def _sigmoid(x):
    return 1.0 / (1.0 + jnp.exp(-x))


def _softplus(x):
    return jnp.maximum(x, 0.0) + jnp.log(1.0 + jnp.exp(-jnp.abs(x)))


def _norm_matmul_kernel(x_ref, g_ref, w_ref, o_ref, h_ref):
    @pl.when(pl.program_id(1) == 0)
    def _():
        h_ref[...] = _rms(x_ref[...], g_ref[...]).astype(BF16)

    o_ref[...] = lax.dot_general(h_ref[...], w_ref[...], NT_DIMS, preferred_element_type=F32)


def _norm_matmul(x, g, wt, tm, tn):
    m, k = x.shape
    n = wt.shape[0]
    return pl.pallas_call(
        _norm_matmul_kernel,
        grid=(m // tm, n // tn),
        in_specs=[pl.BlockSpec((tm, k), lambda i, j: (i, 0)),
                  pl.BlockSpec((1, k), lambda i, j: (0, 0)),
                  pl.BlockSpec((tn, k), lambda i, j: (j, 0))],
        out_specs=pl.BlockSpec((tm, tn), lambda i, j: (i, j)),
        out_shape=jax.ShapeDtypeStruct((m, n), F32),
        scratch_shapes=[pltpu.VMEM((tm, k), BF16)],
        compiler_params=_params(("arbitrary", "arbitrary"), 56),
        name="in_proj",
    )(x, g, wt)


def _rope_pairs(y, cos, sin):
    lane = lax.broadcasted_iota(jnp.int32, (1, LANES), 1)
    first_half = (lane % QK_ROPE) < (QK_ROPE // 2)
    swapped = jnp.where(first_half, pltpu.roll(y, LANES - QK_ROPE // 2, 1), pltpu.roll(y, QK_ROPE // 2, 1))
    return y * cos + swapped * sin


def _mla_pre_kernel(qa_ref, kva_ref, kr_ref, cos_ref, sin_ref, gq_ref, gkv_ref, wq_ref, gqn_ref, gqr_ref,
                    gkr_ref, *rest, scale, with_kv):
    if with_kv:
        wk_ref, wv_ref, gk_ref, ckv_ref, krope_ref, qf_ref, kf_ref, v_ref = rest
    else:
        ckv_ref, krope_ref, qf_ref = rest
    cos = cos_ref[...]
    sin = sin_ref[...]
    lane = lax.broadcasted_iota(jnp.int32, (1, LANES), 1)
    lo = lane < QK_ROPE

    ckv = _rms(kva_ref[...], gkv_ref[...])
    ckv_ref[...] = ckv

    kr = kr_ref[...]
    ss = jnp.sum(jnp.where(lo, kr * kr, 0.0), axis=-1, keepdims=True)
    kr = kr * lax.rsqrt(ss * (1.0 / QK_ROPE) + EPS) * gkr_ref[...]
    krope = _rope_pairs(kr, cos, sin)[:, :QK_ROPE]
    krope_ref[...] = krope

    if with_kv:
        latb = ckv.astype(BF16)
        kn = jnp.dot(latb, wk_ref[...], preferred_element_type=F32)
        vv = jnp.dot(latb, wv_ref[...], preferred_element_type=F32)
        krb = krope.astype(BF16)
        gk = gk_ref[...]
        for h in range(MLA_HEADS):
            kf_ref[0, h, :, 0:QK_NOPE] = _rms(kn[:, h * QK_NOPE:(h + 1) * QK_NOPE], gk).astype(BF16)
            kf_ref[0, h, :, QK_NOPE:QK_DIM] = krb
            v_ref[0, h] = vv[:, h * V_HEAD:(h + 1) * V_HEAD].astype(BF16)

    hq = _rms(qa_ref[...], gq_ref[...]).astype(BF16)
    q = jnp.dot(hq, wq_ref[...], preferred_element_type=F32)
    gqn = gqn_ref[...] * scale
    for h in range(MLA_HEADS):
        xn = q[:, h * QK_NOPE:(h + 1) * QK_NOPE]
        qf_ref[0, h, :, 0:QK_NOPE] = (_rms(xn, gqn)).astype(BF16)
    gqr = gqr_ref[...] * scale
    rope0 = MLA_HEADS * QK_NOPE
    for p in range(MLA_HEADS // 2):
        xr = q[:, rope0 + p * LANES: rope0 + (p + 1) * LANES]
        sq = xr * xr
        s_lo = jnp.sum(jnp.where(lo, sq, 0.0), axis=-1, keepdims=True)
        s_hi = jnp.sum(jnp.where(lo, 0.0, sq), axis=-1, keepdims=True)
        r = jnp.where(lo, lax.rsqrt(s_lo * (1.0 / QK_ROPE) + EPS), lax.rsqrt(s_hi * (1.0 / QK_ROPE) + EPS))
        ro = _rope_pairs(xr * r * gqr, cos, sin).astype(BF16)
        qf_ref[0, 2 * p, :, QK_NOPE:QK_DIM] = ro[:, :QK_ROPE]
        qf_ref[0, 2 * p + 1, :, QK_NOPE:QK_DIM] = ro[:, QK_ROPE:]


def _mla_pre(proj, cos, sin, g_q_lat, g_kv_lat, wq, g_q_nope, g_q_rope2, g_k_rope2, nb, lb, tm, kv_weights=None):
    m = proj.shape[0]
    per_seq = lb // tm
    n_tab = cos.shape[0] // tm
    scale = float(QK_DIM) ** -0.5 * math.log2(math.e)
    row = lambda i: (i, 0)
    const = lambda i: (0, 0)
    heads = lambda i: (i // per_seq, 0, i % per_seq, 0)
    with_kv = kv_weights is not None
    in_specs = [pl.BlockSpec((tm, Q_LORA), lambda i: (i, COL_QA // Q_LORA)),
                pl.BlockSpec((tm, KV_LORA), lambda i: (i, COL_KVA // KV_LORA)),
                pl.BlockSpec((tm, LANES), lambda i: (i, COL_KR // LANES)),
                pl.BlockSpec((tm, LANES), lambda i: (i % n_tab, 0)),
                pl.BlockSpec((tm, LANES), lambda i: (i % n_tab, 0)),
                pl.BlockSpec((1, Q_LORA), const),
                pl.BlockSpec((1, KV_LORA), const),
                pl.BlockSpec(wq.shape, const),
                pl.BlockSpec((1, QK_NOPE), const),
                pl.BlockSpec((1, LANES), const),
                pl.BlockSpec((1, LANES), const)]
    out_specs = [pl.BlockSpec((tm, KV_LORA), row),
                 pl.BlockSpec((tm, QK_ROPE), row),
                 pl.BlockSpec((1, MLA_HEADS, tm, QK_DIM), heads)]
    out_shape = [jax.ShapeDtypeStruct((m, KV_LORA), F32),
                 jax.ShapeDtypeStruct((m, QK_ROPE), F32),
                 jax.ShapeDtypeStruct((nb, MLA_HEADS, lb, QK_DIM), BF16)]
    operands = [proj, proj, proj, cos, sin, g_q_lat, g_kv_lat, wq, g_q_nope, g_q_rope2, g_k_rope2]
    if with_kv:
        wk, wv, g_k_nope = kv_weights
        in_specs += [pl.BlockSpec(wk.shape, const), pl.BlockSpec(wv.shape, const),
                     pl.BlockSpec((1, QK_NOPE), const)]
        out_specs += [pl.BlockSpec((1, MLA_HEADS, tm, QK_DIM), heads),
                      pl.BlockSpec((1, MLA_HEADS, tm, V_HEAD), heads)]
        out_shape += [jax.ShapeDtypeStruct((nb, MLA_HEADS, lb, QK_DIM), BF16),
                      jax.ShapeDtypeStruct((nb, MLA_HEADS, lb, V_HEAD), BF16)]
        operands += [wk, wv, g_k_nope]
    return pl.pallas_call(
        functools.partial(_mla_pre_kernel, scale=scale, with_kv=with_kv),
        grid=(m // tm,),
        in_specs=in_specs,
        out_specs=out_specs,
        out_shape=out_shape,
        compiler_params=_params(("arbitrary",), 40),
        name="mla_pre",
    )(*operands)


def _flash_kernel(q_ref, k_ref, v_ref, o_ref, m_ref, l_ref, acc_ref, *, tq, tk, sub):
    qi = pl.program_id(2)
    m_ref[...] = jnp.full(m_ref.shape, -jnp.inf, F32)
    l_ref[...] = jnp.zeros(l_ref.shape, F32)
    acc_ref[...] = jnp.zeros(acc_ref.shape, F32)

    def attend_rows(r0, nr, keys, k0, masked):
        rows = pl.ds(r0, nr)
        s = lax.dot_general(q_ref[0, 0, rows, :], k_ref[0, 0, keys, :], NT_DIMS, preferred_element_type=F32)
        if masked:
            rc = (r0 + lax.broadcasted_iota(jnp.int32, (nr, tk), 0)) // CHUNK
            cc = (k0 + lax.broadcasted_iota(jnp.int32, (nr, tk), 1)) // CHUNK
            s = jnp.where(cc <= rc, s, -jnp.inf)
        m = m_ref[rows, :]
        m_new = jnp.maximum(m, jnp.max(s, axis=-1, keepdims=True))
        alpha = jnp.exp2(m - m_new)
        pc = [jnp.exp2(s[:, c * LANES:(c + 1) * LANES] - m_new) for c in range(tk // LANES)]
        psum = pc[0]
        for c in range(1, tk // LANES):
            psum = psum + pc[c]
        p = jnp.concatenate(pc, axis=1)
        m_ref[rows, :] = m_new
        l_ref[rows, :] = alpha * l_ref[rows, :] + psum
        acc_ref[rows, :] = alpha * acc_ref[rows, :] + jnp.dot(p.astype(BF16), v_ref[0, 0, keys, :],
                                                              preferred_element_type=F32)

    def attend(r_lo, r_hi, ki, k0, masked):
        keys = pl.ds(pl.multiple_of(ki * tk, tk), tk)
        for r0 in range(r_lo, r_hi, sub):
            if not masked or r0 + sub > k0:
                attend_rows(r0, sub, keys, k0, masked and r0 < k0 + tk)

    def body(ki, c):
        attend(0, tq, ki, 0, False)
        return c

    n_diag = tq // tk
    lax.fori_loop(0, n_diag * qi, body, 0)
    for j in range(n_diag):
        attend(0, tq, n_diag * qi + j, j * tk, True)
    l = jnp.sum(l_ref[...], axis=-1, keepdims=True)
    o_ref[0] = (acc_ref[...] / l).astype(BF16)


def _flash_attention(qf, kf, v, tq, tk, sub):
    nb, nh, lq, _ = qf.shape
    t = kf.shape[2]
    return pl.pallas_call(
        functools.partial(_flash_kernel, tq=tq, tk=tk, sub=sub),
        grid=(nb, nh, lq // tq),
        in_specs=[pl.BlockSpec((1, 1, tq, QK_DIM), lambda b, h, i: (b, h, i, 0)),
                  pl.BlockSpec((1, 1, t, QK_DIM), lambda b, h, i: (b, h, 0, 0)),
                  pl.BlockSpec((1, 1, t, V_HEAD), lambda b, h, i: (b, h, 0, 0))],
        out_specs=pl.BlockSpec((1, tq, V_HEAD), lambda b, h, i: (b, i, h)),
        out_shape=jax.ShapeDtypeStruct((nb, lq, nh * V_HEAD), BF16),
        scratch_shapes=[pltpu.VMEM((tq, LANES), F32), pltpu.VMEM((tq, LANES), F32),
                        pltpu.VMEM((tq, V_HEAD), F32)],
        compiler_params=_params(("arbitrary", "arbitrary", "arbitrary"), 40),
        name="flash_attn",
    )(qf, kf, v)


def _attn_cached_kernel(qn_ref, qr_ref, lat_ref, kr_ref, latn_ref, krn_ref, wk_ref, wv_ref, gk_ref, o_ref,
                        m_ref, l_ref, acc_ref, *, lq):
    kt = pl.program_id(1)

    @pl.when(kt == 0)
    def _():
        m_ref[...] = jnp.full(m_ref.shape, -jnp.inf, F32)
        l_ref[...] = jnp.zeros(l_ref.shape, F32)
        acc_ref[...] = jnp.zeros(acc_ref.shape, F32)

    def attend(lat, kr_t):
        latb = lat.astype(BF16)
        kn = jnp.dot(latb, wk_ref[...], preferred_element_type=F32)
        gk = gk_ref[...]
        knb = jnp.concatenate([_rms(kn[:, h * QK_NOPE:(h + 1) * QK_NOPE], gk).astype(BF16)
                               for h in range(MLA_HEADS)], axis=1)
        s = (lax.dot_general(qn_ref[0], knb, NT_DIMS, preferred_element_type=F32)
             + jnp.dot(qr_ref[0], kr_t.astype(BF16), preferred_element_type=F32))
        m = m_ref[...]
        m_new = jnp.maximum(m, jnp.max(s, axis=-1, keepdims=True))
        alpha = jnp.exp2(m - m_new)
        p = jnp.exp2(s - m_new[:, 0:1])
        m_ref[...] = m_new
        l_ref[...] = alpha * l_ref[...] + jnp.sum(p, axis=-1, keepdims=True)
        acc_ref[...] = alpha[:, 0:1] * acc_ref[...] + jnp.dot(p.astype(BF16), latb, preferred_element_type=F32)

    attend(lat_ref[0], kr_ref[0])

    @pl.when(kt == pl.num_programs(1) - 1)
    def _():
        attend(latn_ref[...], krn_ref[0])
        o_lat = (acc_ref[...] / l_ref[:, 0:1]).astype(BF16)
        for h in range(MLA_HEADS):
            o_ref[:, h * V_HEAD:(h + 1) * V_HEAD] = jnp.dot(
                o_lat[h * lq:(h + 1) * lq, :], wv_ref[:, h * V_HEAD:(h + 1) * V_HEAD],
                preferred_element_type=F32).astype(BF16)


def _attn_cached(qn_bd, qr, lat_past, kr_past, lat_new, kr_new, wk, wv, g_k_nope, lq, tk):
    nb, past, _ = lat_past.shape
    nrow = MLA_HEADS * lq
    kr_past_t = kr_past.transpose(0, 2, 1)
    kr_new_t = kr_new.reshape(nb, lq, QK_ROPE).transpose(0, 2, 1)
    per_b = lambda b, k: (b, 0, 0)
    tile = lambda b, k: (b, k, 0)
    new = lambda b, k: (b, 0)
    const = lambda b, k: (0, 0)
    return pl.pallas_call(
        functools.partial(_attn_cached_kernel, lq=lq),
        grid=(nb, past // tk),
        in_specs=[pl.BlockSpec((1, nrow, MLA_HEADS * QK_NOPE), per_b),
                  pl.BlockSpec((1, nrow, QK_ROPE), per_b),
                  pl.BlockSpec((1, tk, KV_LORA), tile),
                  pl.BlockSpec((1, QK_ROPE, tk), lambda b, k: (b, 0, k)),
                  pl.BlockSpec((lq, KV_LORA), new),
                  pl.BlockSpec((1, QK_ROPE, lq), per_b),
                  pl.BlockSpec(wk.shape, const),
                  pl.BlockSpec(wv.shape, const),
                  pl.BlockSpec((1, QK_NOPE), const)],
        out_specs=pl.BlockSpec((lq, MLA_HEADS * V_HEAD), new),
        out_shape=jax.ShapeDtypeStruct((nb * lq, MLA_HEADS * V_HEAD), BF16),
        scratch_shapes=[pltpu.VMEM((nrow, LANES), F32), pltpu.VMEM((nrow, LANES), F32),
                        pltpu.VMEM((nrow, KV_LORA), F32)],
        compiler_params=_params(("arbitrary", "arbitrary"), 40),
        name="attn_cached",
    )(qn_bd, qr, lat_past, kr_past_t, lat_new, kr_new_t, wk, wv, g_k_nope)


def _gdn_kernel(qkv_ref, z_ref, abc_ref, abr_ref, tail0_ref, s0_ref, wc_ref, alog_c_ref, dt_c_ref, alog_r_ref,
                dt_r_ref, gout_ref, o_ref, sfin_ref, ext_ref, s_ref, *, blk, nsub, valid, n_levels):
    t = pl.program_id(1)
    nt = pl.num_programs(1)
    halo = SUBLANES
    rows = nsub * blk
    n_real = rows if valid == blk else valid

    @pl.when(t == 0)
    def _():
        ext_ref[0:halo, :] = tail0_ref[0]
        s_ref[...] = s0_ref[0]

    @pl.when(t > 0)
    def _():
        ext_ref[0:halo, :] = ext_ref[rows:rows + halo, :]

    def pad_rows(x, to=rows):
        return x if x.shape[0] == to else jnp.concatenate(
            [x, jnp.zeros((to - x.shape[0], x.shape[1]), x.dtype)], axis=0)

    conv_rows = -(-n_real // SUBLANES) * SUBLANES
    ext_ref[halo:halo + conv_rows, :] = pad_rows(qkv_ref[0], conv_rows)
    wc = wc_ref[...]
    conv = wc[GDN_CONV - 1:GDN_CONV, :] * ext_ref[halo:halo + conv_rows, :]
    for i in range(1, GDN_CONV):
        conv = conv + wc[GDN_CONV - 1 - i:GDN_CONV - i, :] * ext_ref[halo - i:halo - i + conv_rows, :]
    act = pad_rows(conv * _sigmoid(conv))

    abc = pad_rows(abc_ref[0])
    abr = abr_ref[0, 0]
    rvalid = lax.broadcasted_iota(jnp.int32, (rows, 1), 0) < n_real
    cvalid = lax.broadcasted_iota(jnp.int32, (1, rows), 1) < n_real
    g_col = jnp.where(rvalid, -jnp.exp(alog_c_ref[...]) * _softplus(abc + dt_c_ref[...]), 0.0)
    beta_col = jnp.where(rvalid, _sigmoid(abc), 0.0)
    g_row = jnp.where(cvalid, -jnp.exp(alog_r_ref[...]) * _softplus(abr + dt_r_ref[...]), 0.0)
    ii = lax.broadcasted_iota(jnp.int32, (blk, blk), 0)
    jj = lax.broadcasted_iota(jnp.int32, (blk, blk), 1)
    incl = ii >= jj
    lower = incl.astype(F32)
    upper = (ii <= jj).astype(F32)
    gc_cols = [jnp.dot(lower, g_col[sb * blk:(sb + 1) * blk], preferred_element_type=F32,
                       precision=lax.Precision.HIGHEST) for sb in range(nsub)]
    gc_rows = [jnp.dot(g_row[:, sb * blk:(sb + 1) * blk], upper, preferred_element_type=F32,
                       precision=lax.Precision.HIGHEST) for sb in range(nsub)]
    eye = (ii == jj).astype(F32)
    merge_masks = []
    for lvl in range(n_levels):
        half = 1 << lvl
        merge_masks.append((ii // (2 * half) == jj // (2 * half)) & ((ii // half) % 2 == 1)
                           & ((jj // half) % 2 == 0))
    nk = GDN_HEADS * GDN_DK
    gout = gout_ref[...]

    heads = range(GDN_HEADS)
    units = [(sb, h) for sb in range(nsub) for h in heads]
    s_cur = [s_ref[h] for h in heads]
    n_out = blk if n_real == rows else n_real
    zs = {(sb, h): z_ref[0, sb * blk:sb * blk + n_out, h * GDN_DV:(h + 1) * GDN_DV] for sb, h in units}
    q, k, v, gc, beta, decay = {}, {}, {}, {}, {}, {}
    for sb, h in units:
        rs = slice(sb * blk, (sb + 1) * blk)
        qh = act[rs, h * GDN_DK:(h + 1) * GDN_DK]
        kh = act[rs, nk + h * GDN_DK: nk + (h + 1) * GDN_DK]
        u = (sb, h)
        q[u] = qh * lax.rsqrt(jnp.sum(qh * qh, axis=-1, keepdims=True) + EPS) * (float(GDN_DK) ** -0.5)
        k[u] = kh * lax.rsqrt(jnp.sum(kh * kh, axis=-1, keepdims=True) + EPS)
        v[u] = act[rs, 2 * nk + h * GDN_DV: 2 * nk + (h + 1) * GDN_DV]
        gc[u] = gc_cols[sb][:, h:h + 1]
        beta[u] = beta_col[rs, GDN_HEADS + h:GDN_HEADS + h + 1]
        decay[u] = jnp.where(incl, jnp.exp(gc[u] - gc_rows[sb][h:h + 1, :]), 0.0)
    kb = {u: k[u].astype(BF16) for u in units}
    kq = {u: lax.dot_general(jnp.concatenate([kb[u], q[u].astype(BF16)], axis=0), kb[u], NT_DIMS,
                             preferred_element_type=F32) for u in units}
    a = {u: beta[u] * kq[u][0:blk] * decay[u] for u in units}
    tinv = {u: eye - jnp.where(merge_masks[0], a[u], 0.0) for u in units}
    n_merge = n_levels if n_real == rows else min(n_levels, max(1, math.ceil(math.log2(n_real))))
    for lvl in range(1, n_merge):
        tb = {u: tinv[u].astype(BF16) for u in units}
        y = {u: jnp.dot(jnp.where(merge_masks[lvl], a[u], 0.0).astype(BF16), tb[u], preferred_element_type=F32)
             for u in units}
        tinv = {u: tinv[u] - jnp.dot(tb[u], y[u].astype(BF16), preferred_element_type=F32) for u in units}
    egc_all = [jnp.exp(gc_cols[sb]) for sb in range(nsub)]
    last_all = [gc_cols[sb][blk - 1:blk, :] for sb in range(nsub)]
    kdec_all = [jnp.exp(last_all[sb] - gc_cols[sb]) for sb in range(nsub)]
    slast_all = [jnp.exp(last_all[sb]) for sb in range(nsub)]
    egc = {(sb, h): egc_all[sb][:, h:h + 1] for sb, h in units}
    uw = {u: jnp.dot(tinv[u].astype(BF16),
                     jnp.concatenate([v[u] * beta[u], k[u] * (beta[u] * egc[u])], axis=1).astype(BF16),
                     preferred_element_type=F32) for u in units}
    o = {}
    for sb in range(nsub):
        sbf = [x.astype(BF16) for x in s_cur]
        ws = [jnp.dot(jnp.concatenate([uw[sb, h][:, GDN_DV:], q[sb, h] * egc[sb, h]], axis=0).astype(BF16),
                      sbf[h], preferred_element_type=F32) for h in heads]
        vb = [(uw[sb, h][:, 0:GDN_DV] - ws[h][0:blk]).astype(BF16) for h in heads]
        for h in heads:
            o[sb, h] = ws[h][blk:2 * blk] + jnp.dot((kq[sb, h][blk:2 * blk] * decay[sb, h]).astype(BF16), vb[h],
                                                    preferred_element_type=F32)
        s_cur = [s_cur[h] * slast_all[sb][:, h:h + 1]
                 + lax.dot_general((k[sb, h] * kdec_all[sb][:, h:h + 1]).astype(BF16), vb[h], TN_DIMS,
                                   preferred_element_type=F32) for h in heads]
    for h in heads:
        s_ref[h] = s_cur[h]
    for sb, h in units:
        o_ref[0, sb * blk:sb * blk + n_out, h * GDN_DV:(h + 1) * GDN_DV] = (
            _rms(o[sb, h][0:n_out], gout) * (zs[sb, h] * _sigmoid(zs[sb, h]))).astype(BF16)

    @pl.when(t == nt - 1)
    def _():
        sfin_ref[0] = s_ref[...]


def _gdn(proj3, tail0, s0, wc, a_log, dt_bias, g_out):
    nb, t, _ = proj3.shape
    blk = GDN_BLOCK
    valid = min(t, blk)
    nsub = GDN_BLOCKS_PER_STEP if t % (GDN_BLOCKS_PER_STEP * blk) == 0 else 1
    rin = nsub * valid
    rows = nsub * blk
    ab = proj3[:, :, COL_AB:COL_AB + 2 * GDN_HEADS].reshape(nb, t // rin, rin, 2 * GDN_HEADS)
    ab_rows = jnp.pad(ab.transpose(0, 1, 3, 2), ((0, 0), (0, 0), (0, 0), (0, rows - rin)))
    pad_c = lambda v: jnp.zeros((1, LANES), F32).at[0, :GDN_HEADS].set(v)
    pad_r = lambda v: jnp.zeros((2 * GDN_HEADS, 1), F32).at[:GDN_HEADS, 0].set(v)
    const2 = lambda b, i: (0, 0)
    return pl.pallas_call(
        functools.partial(_gdn_kernel, blk=blk, nsub=nsub, valid=valid, n_levels=int(math.log2(blk))),
        grid=(nb, t // rin),
        in_specs=[pl.BlockSpec((1, rin, GDN_CONV_DIM), lambda b, i: (b, i, COL_QKV // GDN_CONV_DIM)),
                  pl.BlockSpec((1, rin, GDN_OUT), lambda b, i: (b, i, COL_Z // GDN_OUT)),
                  pl.BlockSpec((1, rin, LANES), lambda b, i: (b, i, COL_AB // LANES)),
                  pl.BlockSpec((1, 1, 2 * GDN_HEADS, rows), lambda b, i: (b, i, 0, 0)),
                  pl.BlockSpec((1, SUBLANES, GDN_CONV_DIM), lambda b, i: (b, 0, 0)),
                  pl.BlockSpec((1, GDN_HEADS, GDN_DK, GDN_DV), lambda b, i: (b, 0, 0, 0)),
                  pl.BlockSpec((GDN_CONV, GDN_CONV_DIM), const2),
                  pl.BlockSpec((1, LANES), const2),
                  pl.BlockSpec((1, LANES), const2),
                  pl.BlockSpec((2 * GDN_HEADS, 1), const2),
                  pl.BlockSpec((2 * GDN_HEADS, 1), const2),
                  pl.BlockSpec((1, GDN_DV), const2)],
        out_specs=[pl.BlockSpec((1, rin, GDN_OUT), lambda b, i: (b, i, 0)),
                   pl.BlockSpec((1, GDN_HEADS, GDN_DK, GDN_DV), lambda b, i: (b, 0, 0, 0))],
        out_shape=[jax.ShapeDtypeStruct((nb, t, GDN_OUT), BF16),
                   jax.ShapeDtypeStruct((nb, GDN_HEADS, GDN_DK, GDN_DV), F32)],
        scratch_shapes=[pltpu.VMEM((rows + 2 * SUBLANES, GDN_CONV_DIM), F32),
                        pltpu.VMEM((GDN_HEADS, GDN_DK, GDN_DV), F32)],
        compiler_params=_params(("arbitrary", "arbitrary"), 40),
        name="gdn",
    )(proj3, proj3, proj3, ab_rows, tail0, s0, wc, pad_c(a_log), pad_c(dt_bias), pad_r(a_log), pad_r(dt_bias),
      g_out)


def _out_proj_kernel(oa_ref, ob_ref, w_ref, x_ref, y_ref):
    y_ref[...] = (x_ref[...]
                  + jnp.dot(oa_ref[...], w_ref[0:MLA_OUT, :], preferred_element_type=F32)
                  + jnp.dot(ob_ref[...], w_ref[MLA_OUT:MLA_OUT + GDN_OUT, :], preferred_element_type=F32))


def _out_proj(o_a, o_b, w, x, tm):
    m = x.shape[0]
    row = lambda i: (i, 0)
    return pl.pallas_call(
        _out_proj_kernel,
        grid=(m // tm,),
        in_specs=[pl.BlockSpec((tm, MLA_OUT), row),
                  pl.BlockSpec((tm, GDN_OUT), row),
                  pl.BlockSpec(w.shape, lambda i: (0, 0)),
                  pl.BlockSpec((tm, D_MODEL), row)],
        out_specs=pl.BlockSpec((tm, D_MODEL), row),
        out_shape=jax.ShapeDtypeStruct((m, D_MODEL), F32),
        compiler_params=_params(("arbitrary",), 48),
        name="out_proj",
    )(o_a, o_b, w, x)


def _ffn_kernel(x_ref, g_ref, halo0_ref, wg_ref, wu_ref, wc_ref, bc_ref, wd_ref, y_ref, st_ref, h_ref, ext_ref,
                carry_ref, *, tm, rows, halo, shift, per_seq):
    i = pl.program_id(0)
    f = pl.program_id(1)

    @pl.when(f == 0)
    def _():
        x = x_ref[...]
        h_ref[...] = _rms(x, g_ref[...]).astype(BF16)
        y_ref[...] = x

    @pl.when(i % per_seq == 0)
    def _():
        ext_ref[0:halo, :] = halo0_ref[0]

    @pl.when(i % per_seq != 0)
    def _():
        ext_ref[0:halo, :] = carry_ref[f]

    wc = wc_ref[...]
    bc = bc_ref[...]
    for r0 in range(0, tm, rows):
        h = h_ref[r0:r0 + rows, :]
        gate = jnp.dot(h, wg_ref[...], preferred_element_type=F32)
        up = jnp.dot(h, wu_ref[...], preferred_element_type=F32)
        ext_ref[halo + r0:halo + r0 + rows, :] = gate
        gc = (wc[2:3, :] * gate + wc[1:2, :] * ext_ref[halo + r0 - shift:halo + r0 - shift + rows, :]
              + wc[0:1, :] * ext_ref[halo + r0 - 2 * shift:halo + r0 - 2 * shift + rows, :] + bc)
        act = (gc * _sigmoid(gc)) * up
        y_ref[r0:r0 + rows, :] += jnp.dot(act.astype(BF16), wd_ref[...], preferred_element_type=F32)
    last = ext_ref[tm:tm + halo, :]
    carry_ref[f] = last
    st_ref[0] = last


def _ffn(x, g, halo0, wg, wu, wc, bc, wd, tm, tf, halo, shift, per_seq):
    m = x.shape[0]
    nf = D_FF // tf
    return pl.pallas_call(
        functools.partial(_ffn_kernel, tm=tm, rows=min(tm, 512), halo=halo, shift=shift, per_seq=per_seq),
        grid=(m // tm, nf),
        in_specs=[pl.BlockSpec((tm, D_MODEL), lambda i, f: (i, 0), pipeline_mode=pl.Buffered(1)),
                  pl.BlockSpec((1, D_MODEL), lambda i, f: (0, 0)),
                  pl.BlockSpec((1, halo, tf), lambda i, f: (i // per_seq, 0, f)),
                  pl.BlockSpec((D_MODEL, tf), lambda i, f: (0, f)),
                  pl.BlockSpec((D_MODEL, tf), lambda i, f: (0, f)),
                  pl.BlockSpec((FFN_CONV, tf), lambda i, f: (0, f)),
                  pl.BlockSpec((1, tf), lambda i, f: (0, f)),
                  pl.BlockSpec((tf, D_MODEL), lambda i, f: (f, 0))],
        out_specs=[pl.BlockSpec((tm, D_MODEL), lambda i, f: (i, 0)),
                   pl.BlockSpec((1, halo, tf), lambda i, f: (i, 0, f))],
        out_shape=[jax.ShapeDtypeStruct((m, D_MODEL), F32),
                   jax.ShapeDtypeStruct((m // tm, halo, D_FF), F32)],
        scratch_shapes=[pltpu.VMEM((tm, D_MODEL), BF16),
                        pltpu.VMEM((halo + tm, tf), F32),
                        pltpu.VMEM((nf, halo, tf), F32)],
        compiler_params=_params(("arbitrary", "arbitrary"), 60),
        name="conv_ffn",
    )(x, g, halo0, wg, wu, wc, bc, wd)


def _rope_tables(pos, reps):
    half = QK_ROPE // 2
    inv = 1.0 / (ROPE_THETA ** (jnp.arange(half, dtype=F32) / half))
    ang = pos.astype(F32)[:, None] * inv[None, :]
    cos, sin = jnp.cos(ang), jnp.sin(ang)
    cos = jnp.tile(jnp.concatenate([cos, cos], axis=-1), (reps, LANES // QK_ROPE))
    sin = jnp.tile(jnp.concatenate([-sin, sin], axis=-1), (reps, LANES // QK_ROPE))
    return cos, sin


def _prep_weights(lw):
    w_in = lw['w_in']
    off = np.cumsum([Q_LORA, KV_LORA, QK_ROPE, GDN_CONV_DIM, GDN_OUT, GDN_HEADS, GDN_HEADS]).tolist()
    wt = w_in.T
    zr = lambda n: jnp.zeros((n, D_MODEL), w_in.dtype)
    w_in_r = jnp.concatenate([wt[off[2]:off[4]], wt[:off[1]], wt[off[1]:off[2]], zr(LANES - QK_ROPE),
                              wt[off[4]:off[6]], zr(LANES - 2 * GDN_HEADS)], axis=0)
    wq = lw['w_q_up'].reshape(Q_LORA, MLA_HEADS, QK_DIM)
    wq_r = jnp.concatenate([wq[:, :, :QK_NOPE].reshape(Q_LORA, -1), wq[:, :, QK_NOPE:].reshape(Q_LORA, -1)], axis=1)
    wkv = lw['w_kv_up']
    row = lambda v: v.reshape(1, -1).astype(F32)
    return dict(
        w_in=w_in_r.astype(BF16), wq=wq_r.astype(BF16),
        wk=wkv[:, :, :QK_NOPE].reshape(KV_LORA, -1).astype(BF16),
        wv=wkv[:, :, QK_NOPE:].reshape(KV_LORA, -1).astype(BF16),
        w_out=lw['w_out'].astype(BF16), wg=lw['w_ffn_gate'].astype(BF16), wu=lw['w_ffn_up'].astype(BF16),
        wd=lw['w_ffn_down'].astype(BF16),
        g_attn=row(lw['g_attn_norm']), g_q_lat=row(lw['g_q_lat']), g_kv_lat=row(lw['g_kv_lat']),
        g_q_nope=row(lw['g_q_nope']), g_k_nope=row(lw['g_k_nope']),
        g_q_rope2=row(jnp.tile(lw['g_q_rope'], LANES // QK_ROPE)),
        g_k_rope2=row(jnp.tile(lw['g_k_rope'], LANES // QK_ROPE)),
        wc_gdn=lw['w_gdn_conv'].astype(F32), a_log=lw['a_log'].astype(F32), dt_bias=lw['dt_bias'].astype(F32),
        g_gdn_out=row(lw['g_gdn_out']), g_ffn=row(lw['g_ffn_norm']), wc_ffn=lw['w_ffn_conv'].astype(F32),
        bc_ffn=row(lw['b_ffn_conv']))


def _pad_rows_front(a, rows):
    return jnp.pad(a, ((0, 0), (rows - a.shape[1], 0), (0, 0)))


def _prompt_layer(x, w):
    nb, lb, _ = x.shape
    m = nb * lb
    xf = x.reshape(m, D_MODEL)
    proj = _norm_matmul(xf, w['g_attn'], w['w_in'], 1024, 1792)
    cos, sin = _rope_tables(jnp.arange(lb), 1)
    c_kv, k_rope, qf, kf, v = _mla_pre(proj, cos, sin, w['g_q_lat'], w['g_kv_lat'], w['wq'], w['g_q_nope'],
                                       w['g_q_rope2'], w['g_k_rope2'], nb, lb, 512,
                                       kv_weights=(w['wk'], w['wv'], w['g_k_nope']))
    o_a = _flash_attention(qf, kf, v, 4096, 1024, 1024).reshape(m, MLA_OUT)

    proj3 = proj.reshape(nb, lb, D_IN_PAD)
    tail0 = jnp.zeros((nb, SUBLANES, GDN_CONV_DIM), F32)
    s0 = jnp.zeros((nb, GDN_HEADS, GDN_DK, GDN_DV), F32)
    o_b, s_new = _gdn(proj3, tail0, s0, w['wc_gdn'], w['a_log'], w['dt_bias'], w['g_gdn_out'])
    x1 = _out_proj(o_a, o_b.reshape(m, GDN_OUT), w['w_out'], xf, 512)

    tm = 1024
    halo0 = jnp.zeros((nb, SUBLANES, D_FF), F32)
    y, gate_tail = _ffn(x1, w['g_ffn'], halo0, w['wg'], w['wu'], w['wc_ffn'], w['bc_ffn'], w['wd'],
                        tm, 512, SUBLANES, 1, lb // tm)
    state = (c_kv.reshape(nb, lb, KV_LORA), k_rope.reshape(nb, lb, QK_ROPE),
             proj3[:, lb - (GDN_CONV - 1):, COL_QKV:COL_QKV + GDN_CONV_DIM], s_new,
             gate_tail.reshape(nb, lb // tm, SUBLANES, D_FF)[:, -1, SUBLANES - (FFN_CONV - 1):, :])
    return y.reshape(nb, lb, D_MODEL), state


def _sample_layer(x, lat_past, krope_past, conv_past, s_past, ffn_past, w):
    nb, lb, _ = x.shape
    past = lat_past.shape[1]
    assert (past + lb - 1) // CHUNK == past // CHUNK and past % CHUNK == 0, "new frames must share one chunk"
    m = nb * lb
    xf = x.reshape(m, D_MODEL)
    proj = _norm_matmul(xf, w['g_attn'], w['w_in'], m, 768)
    cos, sin = _rope_tables(past + jnp.arange(lb), nb)
    c_kv, k_rope, qf = _mla_pre(proj, cos, sin, w['g_q_lat'], w['g_kv_lat'], w['wq'], w['g_q_nope'],
                                w['g_q_rope2'], w['g_k_rope2'], 1, m, m)
    qh = qf[0].reshape(MLA_HEADS, nb, lb, QK_DIM).transpose(1, 0, 2, 3)
    qn_bd = jnp.einsum('bhqd,hg->bhqgd', qh[..., :QK_NOPE], jnp.eye(MLA_HEADS, dtype=qh.dtype))
    qn_bd = qn_bd.reshape(nb, MLA_HEADS * lb, MLA_HEADS * QK_NOPE)
    qr = qh[..., QK_NOPE:].reshape(nb, MLA_HEADS * lb, QK_ROPE)
    o_a = _attn_cached(qn_bd, qr, lat_past, krope_past, c_kv, k_rope, w['wk'], w['wv'], w['g_k_nope'], lb, 2048)

    proj3 = proj.reshape(nb, lb, D_IN_PAD)
    tail0 = _pad_rows_front(conv_past.astype(F32), SUBLANES)
    o_b, s_new = _gdn(proj3, tail0, s_past.astype(F32), w['wc_gdn'], w['a_log'], w['dt_bias'], w['g_gdn_out'])
    x1 = _out_proj(o_a, o_b.reshape(m, GDN_OUT), w['w_out'], xf, m)

    x1t = x1.reshape(nb, lb, D_MODEL).transpose(1, 0, 2).reshape(m, D_MODEL)
    n_hist = FFN_CONV - 1
    halo0 = ffn_past.astype(F32).transpose(1, 0, 2).reshape(1, n_hist * nb, D_FF)
    yt, gate_tail = _ffn(x1t, w['g_ffn'], halo0, w['wg'], w['wu'], w['wc_ffn'], w['bc_ffn'], w['wd'],
                         m, 512, n_hist * nb, nb, 1)
    y = yt.reshape(lb, nb, D_MODEL).transpose(1, 0, 2)
    state = (c_kv.reshape(nb, lb, KV_LORA), k_rope.reshape(nb, lb, QK_ROPE),
             proj3[:, lb - (GDN_CONV - 1):, COL_QKV:COL_QKV + GDN_CONV_DIM], s_new,
             gate_tail.reshape(n_hist, nb, D_FF).transpose(1, 0, 2))
    return y, state


def kernel(x_prompt, x_sample, cache_mla_latent, cache_mla_krope, state_gdn_conv, state_gdn_S, state_ffn_conv,
           g_attn_norm, w_in, g_q_lat, g_kv_lat, w_q_up, w_kv_up, g_q_nope, g_q_rope, g_k_nope, g_k_rope,
           w_gdn_conv, a_log, dt_bias, g_gdn_out, w_out, g_ffn_norm, w_ffn_gate, w_ffn_up, w_ffn_conv,
           b_ffn_conv, w_ffn_down):
    xp, xs = x_prompt, x_sample
    new_p, new_s = [], []
    for l in range(w_in.shape[0]):
        w = _prep_weights(dict(
            g_attn_norm=g_attn_norm[l], w_in=w_in[l], g_q_lat=g_q_lat[l], g_kv_lat=g_kv_lat[l], w_q_up=w_q_up[l],
            w_kv_up=w_kv_up[l], g_q_nope=g_q_nope[l], g_q_rope=g_q_rope[l], g_k_nope=g_k_nope[l],
            g_k_rope=g_k_rope[l], w_gdn_conv=w_gdn_conv[l], a_log=a_log[l], dt_bias=dt_bias[l],
            g_gdn_out=g_gdn_out[l], w_out=w_out[l], g_ffn_norm=g_ffn_norm[l], w_ffn_gate=w_ffn_gate[l],
            w_ffn_up=w_ffn_up[l], w_ffn_conv=w_ffn_conv[l], b_ffn_conv=b_ffn_conv[l], w_ffn_down=w_ffn_down[l]))
        xp, st_p = _prompt_layer(xp, w)
        xs, st_s = _sample_layer(xs, cache_mla_latent[l], cache_mla_krope[l], state_gdn_conv[l], state_gdn_S[l],
                                 state_ffn_conv[l], w)
        new_p.append(st_p)
        new_s.append(st_s)
    p_state = [jnp.stack(t) for t in zip(*new_p)]
    s_state = [jnp.stack(t) for t in zip(*new_s)]
    return (xp, xs, *p_state, *s_state)
```

```python
import functools
import math

import jax
import jax.numpy as jnp
import numpy as np
from jax import lax
from jax.experimental import pallas as pl
from jax.experimental.pallas import tpu as pltpu

D_MODEL = 2048
CHUNK = 64
EPS = 1e-6
MLA_HEADS = 8
Q_LORA = 512
KV_LORA = 512
QK_NOPE = 128
QK_ROPE = 64
V_HEAD = 128
ROPE_THETA = 10000.0
GDN_HEADS = 8
GDN_DK = 128
GDN_DV = 128
GDN_CONV = 4
GDN_CONV_DIM = 2 * GDN_HEADS * GDN_DK + GDN_HEADS * GDN_DV
D_FF = 5632
FFN_CONV = 3
MLA_OUT = MLA_HEADS * V_HEAD
GDN_OUT = GDN_HEADS * GDN_DV
QK_DIM = QK_NOPE + QK_ROPE

LANES = 128
SUBLANES = 8
GDN_BLOCK = 128
GDN_BLOCKS_PER_STEP = 2

COL_QKV = 0
COL_Z = COL_QKV + GDN_CONV_DIM
COL_QA = COL_Z + GDN_OUT
COL_KVA = COL_QA + Q_LORA
COL_KR = COL_KVA + KV_LORA
COL_AB = COL_KR + LANES
D_IN_PAD = COL_AB + LANES

IN_PROJ_ROWS = 1024
IN_PROJ_COLS = 1792
IN_PROJ_COLS_SMALL = 768
MLA_ROWS = 512
FLASH_Q_ROWS = 4096
FLASH_KEY_ROWS = 1024
FLASH_ROW_GROUP = 1024
DECODE_KEY_ROWS = 2048
OUT_PROJ_ROWS = 512
FFN_ROWS = 1024
FFN_ROW_CHUNK = 512
FFN_COLS = 512
VMEM_MIB = dict(in_proj=56, mla_pre=40, flash=40, decode=40, gdn=40, out_proj=48, ffn=60)

BF16 = jnp.bfloat16
F32 = jnp.float32
NT_DIMS = (((1,), (1,)), ((), ()))
TN_DIMS = (((0,), (0,)), ((), ()))


def _params(sem, vmem_mb):
    return pltpu.CompilerParams(dimension_semantics=sem, vmem_limit_bytes=vmem_mb * 1024 * 1024)


def _rms(x, g):
    return x * lax.rsqrt(jnp.mean(x * x, axis=-1, keepdims=True) + EPS) * g


def _sigmoid(x):
    return 1.0 / (1.0 + jnp.exp(-x))


def _softplus(x):
    return jnp.maximum(x, 0.0) + jnp.log(1.0 + jnp.exp(-jnp.abs(x)))


def _norm_matmul_kernel(x_ref, g_ref, w_ref, o_ref, h_ref):
    @pl.when(pl.program_id(1) == 0)
    def _():
        h_ref[...] = _rms(x_ref[...], g_ref[...]).astype(BF16)

    o_ref[...] = lax.dot_general(h_ref[...], w_ref[...], NT_DIMS, preferred_element_type=F32)


def _norm_matmul(x, g, wt, tm, tn):
    m, k = x.shape
    n = wt.shape[0]
    return pl.pallas_call(
        _norm_matmul_kernel,
        grid=(m // tm, n // tn),
        in_specs=[pl.BlockSpec((tm, k), lambda i, j: (i, 0)),
                  pl.BlockSpec((1, k), lambda i, j: (0, 0)),
                  pl.BlockSpec((tn, k), lambda i, j: (j, 0))],
        out_specs=pl.BlockSpec((tm, tn), lambda i, j: (i, j)),
        out_shape=jax.ShapeDtypeStruct((m, n), F32),
        scratch_shapes=[pltpu.VMEM((tm, k), BF16)],
        compiler_params=_params(("arbitrary", "arbitrary"), VMEM_MIB['in_proj']),
        name="in_proj",
    )(x, g, wt)


def _rope_pairs(y, cos, sin):
    lane = lax.broadcasted_iota(jnp.int32, (1, LANES), 1)
    first_half = (lane % QK_ROPE) < (QK_ROPE // 2)
    swapped = jnp.where(first_half, pltpu.roll(y, LANES - QK_ROPE // 2, 1), pltpu.roll(y, QK_ROPE // 2, 1))
    return y * cos + swapped * sin


def _mla_pre_kernel(qa_ref, kva_ref, kr_ref, cos_ref, sin_ref, gq_ref, gkv_ref, wq_ref, gqn_ref, gqr_ref,
                    gkr_ref, *rest, scale, with_kv):
    if with_kv:
        wk_ref, wv_ref, gk_ref, ckv_ref, krope_ref, qf_ref, kf_ref, v_ref = rest
    else:
        ckv_ref, krope_ref, qf_ref = rest
    cos = cos_ref[...]
    sin = sin_ref[...]
    lane = lax.broadcasted_iota(jnp.int32, (1, LANES), 1)
    lo = lane < QK_ROPE

    ckv = _rms(kva_ref[...], gkv_ref[...])
    ckv_ref[...] = ckv

    kr = kr_ref[...]
    ss = jnp.sum(jnp.where(lo, kr * kr, 0.0), axis=-1, keepdims=True)
    kr = kr * lax.rsqrt(ss * (1.0 / QK_ROPE) + EPS) * gkr_ref[...]
    krope = _rope_pairs(kr, cos, sin)[:, :QK_ROPE]
    krope_ref[...] = krope

    if with_kv:
        latb = ckv.astype(BF16)
        kn = jnp.dot(latb, wk_ref[...], preferred_element_type=F32)
        vv = jnp.dot(latb, wv_ref[...], preferred_element_type=F32)
        krb = krope.astype(BF16)
        gk = gk_ref[...]
        for h in range(MLA_HEADS):
            kf_ref[0, h, :, 0:QK_NOPE] = _rms(kn[:, h * QK_NOPE:(h + 1) * QK_NOPE], gk).astype(BF16)
            kf_ref[0, h, :, QK_NOPE:QK_DIM] = krb
            v_ref[0, h] = vv[:, h * V_HEAD:(h + 1) * V_HEAD].astype(BF16)

    hq = _rms(qa_ref[...], gq_ref[...]).astype(BF16)
    q = jnp.dot(hq, wq_ref[...], preferred_element_type=F32)
    gqn = gqn_ref[...] * scale
    for h in range(MLA_HEADS):
        xn = q[:, h * QK_NOPE:(h + 1) * QK_NOPE]
        qf_ref[0, h, :, 0:QK_NOPE] = (_rms(xn, gqn)).astype(BF16)
    gqr = gqr_ref[...] * scale
    rope0 = MLA_HEADS * QK_NOPE
    for p in range(MLA_HEADS // 2):
        xr = q[:, rope0 + p * LANES: rope0 + (p + 1) * LANES]
        sq = xr * xr
        s_lo = jnp.sum(jnp.where(lo, sq, 0.0), axis=-1, keepdims=True)
        s_hi = jnp.sum(jnp.where(lo, 0.0, sq), axis=-1, keepdims=True)
        r = jnp.where(lo, lax.rsqrt(s_lo * (1.0 / QK_ROPE) + EPS), lax.rsqrt(s_hi * (1.0 / QK_ROPE) + EPS))
        ro = _rope_pairs(xr * r * gqr, cos, sin).astype(BF16)
        qf_ref[0, 2 * p, :, QK_NOPE:QK_DIM] = ro[:, :QK_ROPE]
        qf_ref[0, 2 * p + 1, :, QK_NOPE:QK_DIM] = ro[:, QK_ROPE:]


def _mla_pre(proj, cos, sin, g_q_lat, g_kv_lat, wq, g_q_nope, g_q_rope2, g_k_rope2, nb, lb, tm, kv_weights=None):
    m = proj.shape[0]
    per_seq = lb // tm
    n_tab = cos.shape[0] // tm
    scale = float(QK_DIM) ** -0.5 * math.log2(math.e)
    row = lambda i: (i, 0)
    const = lambda i: (0, 0)
    heads = lambda i: (i // per_seq, 0, i % per_seq, 0)
    with_kv = kv_weights is not None
    in_specs = [pl.BlockSpec((tm, Q_LORA), lambda i: (i, COL_QA // Q_LORA)),
                pl.BlockSpec((tm, KV_LORA), lambda i: (i, COL_KVA // KV_LORA)),
                pl.BlockSpec((tm, LANES), lambda i: (i, COL_KR // LANES)),
                pl.BlockSpec((tm, LANES), lambda i: (i % n_tab, 0)),
                pl.BlockSpec((tm, LANES), lambda i: (i % n_tab, 0)),
                pl.BlockSpec((1, Q_LORA), const),
                pl.BlockSpec((1, KV_LORA), const),
                pl.BlockSpec(wq.shape, const),
                pl.BlockSpec((1, QK_NOPE), const),
                pl.BlockSpec((1, LANES), const),
                pl.BlockSpec((1, LANES), const)]
    out_specs = [pl.BlockSpec((tm, KV_LORA), row),
                 pl.BlockSpec((tm, QK_ROPE), row),
                 pl.BlockSpec((1, MLA_HEADS, tm, QK_DIM), heads)]
    out_shape = [jax.ShapeDtypeStruct((m, KV_LORA), F32),
                 jax.ShapeDtypeStruct((m, QK_ROPE), F32),
                 jax.ShapeDtypeStruct((nb, MLA_HEADS, lb, QK_DIM), BF16)]
    operands = [proj, proj, proj, cos, sin, g_q_lat, g_kv_lat, wq, g_q_nope, g_q_rope2, g_k_rope2]
    if with_kv:
        wk, wv, g_k_nope = kv_weights
        in_specs += [pl.BlockSpec(wk.shape, const), pl.BlockSpec(wv.shape, const),
                     pl.BlockSpec((1, QK_NOPE), const)]
        out_specs += [pl.BlockSpec((1, MLA_HEADS, tm, QK_DIM), heads),
                      pl.BlockSpec((1, MLA_HEADS, tm, V_HEAD), heads)]
        out_shape += [jax.ShapeDtypeStruct((nb, MLA_HEADS, lb, QK_DIM), BF16),
                      jax.ShapeDtypeStruct((nb, MLA_HEADS, lb, V_HEAD), BF16)]
        operands += [wk, wv, g_k_nope]
    return pl.pallas_call(
        functools.partial(_mla_pre_kernel, scale=scale, with_kv=with_kv),
        grid=(m // tm,),
        in_specs=in_specs,
        out_specs=out_specs,
        out_shape=out_shape,
        compiler_params=_params(("arbitrary",), VMEM_MIB['mla_pre']),
        name="mla_pre",
    )(*operands)


def _flash_kernel(q_ref, k_ref, v_ref, o_ref, m_ref, l_ref, acc_ref, *, tq, tk, sub):
    qi = pl.program_id(2)
    m_ref[...] = jnp.full(m_ref.shape, -jnp.inf, F32)
    l_ref[...] = jnp.zeros(l_ref.shape, F32)
    acc_ref[...] = jnp.zeros(acc_ref.shape, F32)

    def attend_rows(r0, nr, keys, k0, masked):
        rows = pl.ds(r0, nr)
        s = lax.dot_general(q_ref[0, 0, rows, :], k_ref[0, 0, keys, :], NT_DIMS, preferred_element_type=F32)
        if masked:
            rc = (r0 + lax.broadcasted_iota(jnp.int32, (nr, tk), 0)) // CHUNK
            cc = (k0 + lax.broadcasted_iota(jnp.int32, (nr, tk), 1)) // CHUNK
            s = jnp.where(cc <= rc, s, -jnp.inf)
        m = m_ref[rows, :]
        m_new = jnp.maximum(m, jnp.max(s, axis=-1, keepdims=True))
        alpha = jnp.exp2(m - m_new)
        pc = [jnp.exp2(s[:, c * LANES:(c + 1) * LANES] - m_new) for c in range(tk // LANES)]
        psum = pc[0]
        for c in range(1, tk // LANES):
            psum = psum + pc[c]
        p = jnp.concatenate(pc, axis=1)
        m_ref[rows, :] = m_new
        l_ref[rows, :] = alpha * l_ref[rows, :] + psum
        acc_ref[rows, :] = alpha * acc_ref[rows, :] + jnp.dot(p.astype(BF16), v_ref[0, 0, keys, :],
                                                              preferred_element_type=F32)

    def attend(r_lo, r_hi, ki, k0, masked):
        keys = pl.ds(pl.multiple_of(ki * tk, tk), tk)
        for r0 in range(r_lo, r_hi, sub):
            if not masked or r0 + sub > k0:
                attend_rows(r0, sub, keys, k0, masked and r0 < k0 + tk)

    def body(ki, c):
        attend(0, tq, ki, 0, False)
        return c

    n_diag = tq // tk
    lax.fori_loop(0, n_diag * qi, body, 0)
    for j in range(n_diag):
        attend(0, tq, n_diag * qi + j, j * tk, True)
    l = jnp.sum(l_ref[...], axis=-1, keepdims=True)
    o_ref[0] = (acc_ref[...] / l).astype(BF16)


def _flash_attention(qf, kf, v, tq, tk, sub):
    nb, nh, lq, _ = qf.shape
    t = kf.shape[2]
    return pl.pallas_call(
        functools.partial(_flash_kernel, tq=tq, tk=tk, sub=sub),
        grid=(nb, nh, lq // tq),
        in_specs=[pl.BlockSpec((1, 1, tq, QK_DIM), lambda b, h, i: (b, h, i, 0)),
                  pl.BlockSpec((1, 1, t, QK_DIM), lambda b, h, i: (b, h, 0, 0)),
                  pl.BlockSpec((1, 1, t, V_HEAD), lambda b, h, i: (b, h, 0, 0))],
        out_specs=pl.BlockSpec((1, tq, V_HEAD), lambda b, h, i: (b, i, h)),
        out_shape=jax.ShapeDtypeStruct((nb, lq, nh * V_HEAD), BF16),
        scratch_shapes=[pltpu.VMEM((tq, LANES), F32), pltpu.VMEM((tq, LANES), F32),
                        pltpu.VMEM((tq, V_HEAD), F32)],
        compiler_params=_params(("arbitrary", "arbitrary", "arbitrary"), VMEM_MIB['flash']),
        name="flash_attn",
    )(qf, kf, v)


def _attn_cached_kernel(qn_ref, qr_ref, lat_ref, kr_ref, latn_ref, krn_ref, wk_ref, wv_ref, gk_ref, o_ref,
                        m_ref, l_ref, acc_ref, *, lq):
    kt = pl.program_id(1)

    @pl.when(kt == 0)
    def _():
        m_ref[...] = jnp.full(m_ref.shape, -jnp.inf, F32)
        l_ref[...] = jnp.zeros(l_ref.shape, F32)
        acc_ref[...] = jnp.zeros(acc_ref.shape, F32)

    def attend(lat, kr_t):
        latb = lat.astype(BF16)
        kn = jnp.dot(latb, wk_ref[...], preferred_element_type=F32)
        gk = gk_ref[...]
        knb = jnp.concatenate([_rms(kn[:, h * QK_NOPE:(h + 1) * QK_NOPE], gk).astype(BF16)
                               for h in range(MLA_HEADS)], axis=1)
        s = (lax.dot_general(qn_ref[0], knb, NT_DIMS, preferred_element_type=F32)
             + jnp.dot(qr_ref[0], kr_t.astype(BF16), preferred_element_type=F32))
        m = m_ref[...]
        m_new = jnp.maximum(m, jnp.max(s, axis=-1, keepdims=True))
        alpha = jnp.exp2(m - m_new)
        p = jnp.exp2(s - m_new[:, 0:1])
        m_ref[...] = m_new
        l_ref[...] = alpha * l_ref[...] + jnp.sum(p, axis=-1, keepdims=True)
        acc_ref[...] = alpha[:, 0:1] * acc_ref[...] + jnp.dot(p.astype(BF16), latb, preferred_element_type=F32)

    attend(lat_ref[0], kr_ref[0])

    @pl.when(kt == pl.num_programs(1) - 1)
    def _():
        attend(latn_ref[...], krn_ref[0])
        o_lat = (acc_ref[...] / l_ref[:, 0:1]).astype(BF16)
        for h in range(MLA_HEADS):
            o_ref[:, h * V_HEAD:(h + 1) * V_HEAD] = jnp.dot(
                o_lat[h * lq:(h + 1) * lq, :], wv_ref[:, h * V_HEAD:(h + 1) * V_HEAD],
                preferred_element_type=F32).astype(BF16)


def _attn_cached(qn_bd, qr, lat_past, kr_past, lat_new, kr_new, wk, wv, g_k_nope, lq, tk):
    nb, past, _ = lat_past.shape
    nrow = MLA_HEADS * lq
    kr_past_t = kr_past.transpose(0, 2, 1)
    kr_new_t = kr_new.reshape(nb, lq, QK_ROPE).transpose(0, 2, 1)
    per_b = lambda b, k: (b, 0, 0)
    tile = lambda b, k: (b, k, 0)
    new = lambda b, k: (b, 0)
    const = lambda b, k: (0, 0)
    return pl.pallas_call(
        functools.partial(_attn_cached_kernel, lq=lq),
        grid=(nb, past // tk),
        in_specs=[pl.BlockSpec((1, nrow, MLA_HEADS * QK_NOPE), per_b),
                  pl.BlockSpec((1, nrow, QK_ROPE), per_b),
                  pl.BlockSpec((1, tk, KV_LORA), tile),
                  pl.BlockSpec((1, QK_ROPE, tk), lambda b, k: (b, 0, k)),
                  pl.BlockSpec((lq, KV_LORA), new),
                  pl.BlockSpec((1, QK_ROPE, lq), per_b),
                  pl.BlockSpec(wk.shape, const),
                  pl.BlockSpec(wv.shape, const),
                  pl.BlockSpec((1, QK_NOPE), const)],
        out_specs=pl.BlockSpec((lq, MLA_HEADS * V_HEAD), new),
        out_shape=jax.ShapeDtypeStruct((nb * lq, MLA_HEADS * V_HEAD), BF16),
        scratch_shapes=[pltpu.VMEM((nrow, LANES), F32), pltpu.VMEM((nrow, LANES), F32),
                        pltpu.VMEM((nrow, KV_LORA), F32)],
        compiler_params=_params(("arbitrary", "arbitrary"), VMEM_MIB['decode']),
        name="attn_cached",
    )(qn_bd, qr, lat_past, kr_past_t, lat_new, kr_new_t, wk, wv, g_k_nope)


def _gdn_kernel(qkv_ref, z_ref, abc_ref, abr_ref, tail0_ref, s0_ref, wc_ref, alog_c_ref, dt_c_ref, alog_r_ref,
                dt_r_ref, gout_ref, o_ref, sfin_ref, ext_ref, s_ref, *, blk, nsub, valid, n_levels):
    t = pl.program_id(1)
    nt = pl.num_programs(1)
    halo = SUBLANES
    rows = nsub * blk
    n_real = rows if valid == blk else valid

    @pl.when(t == 0)
    def _():
        ext_ref[0:halo, :] = tail0_ref[0]
        s_ref[...] = s0_ref[0]

    @pl.when(t > 0)
    def _():
        ext_ref[0:halo, :] = ext_ref[rows:rows + halo, :]

    def pad_rows(x, to=rows):
        return x if x.shape[0] == to else jnp.concatenate(
            [x, jnp.zeros((to - x.shape[0], x.shape[1]), x.dtype)], axis=0)

    conv_rows = -(-n_real // SUBLANES) * SUBLANES
    ext_ref[halo:halo + conv_rows, :] = pad_rows(qkv_ref[0], conv_rows)
    wc = wc_ref[...]
    conv = wc[GDN_CONV - 1:GDN_CONV, :] * ext_ref[halo:halo + conv_rows, :]
    for i in range(1, GDN_CONV):
        conv = conv + wc[GDN_CONV - 1 - i:GDN_CONV - i, :] * ext_ref[halo - i:halo - i + conv_rows, :]
    act = pad_rows(conv * _sigmoid(conv))

    abc = pad_rows(abc_ref[0])
    abr = abr_ref[0, 0]
    rvalid = lax.broadcasted_iota(jnp.int32, (rows, 1), 0) < n_real
    cvalid = lax.broadcasted_iota(jnp.int32, (1, rows), 1) < n_real
    g_col = jnp.where(rvalid, -jnp.exp(alog_c_ref[...]) * _softplus(abc + dt_c_ref[...]), 0.0)
    beta_col = jnp.where(rvalid, _sigmoid(abc), 0.0)
    g_row = jnp.where(cvalid, -jnp.exp(alog_r_ref[...]) * _softplus(abr + dt_r_ref[...]), 0.0)
    ii = lax.broadcasted_iota(jnp.int32, (blk, blk), 0)
    jj = lax.broadcasted_iota(jnp.int32, (blk, blk), 1)
    incl = ii >= jj
    lower = incl.astype(F32)
    upper = (ii <= jj).astype(F32)
    gc_cols = [jnp.dot(lower, g_col[sb * blk:(sb + 1) * blk], preferred_element_type=F32,
                       precision=lax.Precision.HIGHEST) for sb in range(nsub)]
    gc_rows = [jnp.dot(g_row[:, sb * blk:(sb + 1) * blk], upper, preferred_element_type=F32,
                       precision=lax.Precision.HIGHEST) for sb in range(nsub)]
    eye = (ii == jj).astype(F32)
    merge_masks = []
    for lvl in range(n_levels):
        half = 1 << lvl
        merge_masks.append((ii // (2 * half) == jj // (2 * half)) & ((ii // half) % 2 == 1)
                           & ((jj // half) % 2 == 0))
    nk = GDN_HEADS * GDN_DK
    gout = gout_ref[...]

    heads = range(GDN_HEADS)
    units = [(sb, h) for sb in range(nsub) for h in heads]
    s_cur = [s_ref[h] for h in heads]
    n_out = blk if n_real == rows else n_real
    zs = {(sb, h): z_ref[0, sb * blk:sb * blk + n_out, h * GDN_DV:(h + 1) * GDN_DV] for sb, h in units}
    q, k, v, gc, beta, decay = {}, {}, {}, {}, {}, {}
    for sb, h in units:
        rs = slice(sb * blk, (sb + 1) * blk)
        qh = act[rs, h * GDN_DK:(h + 1) * GDN_DK]
        kh = act[rs, nk + h * GDN_DK: nk + (h + 1) * GDN_DK]
        u = (sb, h)
        q[u] = qh * lax.rsqrt(jnp.sum(qh * qh, axis=-1, keepdims=True) + EPS) * (float(GDN_DK) ** -0.5)
        k[u] = kh * lax.rsqrt(jnp.sum(kh * kh, axis=-1, keepdims=True) + EPS)
        v[u] = act[rs, 2 * nk + h * GDN_DV: 2 * nk + (h + 1) * GDN_DV]
        gc[u] = gc_cols[sb][:, h:h + 1]
        beta[u] = beta_col[rs, GDN_HEADS + h:GDN_HEADS + h + 1]
        decay[u] = jnp.where(incl, jnp.exp(gc[u] - gc_rows[sb][h:h + 1, :]), 0.0)
    kb = {u: k[u].astype(BF16) for u in units}
    kq = {u: lax.dot_general(jnp.concatenate([kb[u], q[u].astype(BF16)], axis=0), kb[u], NT_DIMS,
                             preferred_element_type=F32) for u in units}
    a = {u: beta[u] * kq[u][0:blk] * decay[u] for u in units}
    tinv = {u: eye - jnp.where(merge_masks[0], a[u], 0.0) for u in units}
    n_merge = n_levels if n_real == rows else min(n_levels, max(1, math.ceil(math.log2(n_real))))
    for lvl in range(1, n_merge):
        tb = {u: tinv[u].astype(BF16) for u in units}
        y = {u: jnp.dot(jnp.where(merge_masks[lvl], a[u], 0.0).astype(BF16), tb[u], preferred_element_type=F32)
             for u in units}
        tinv = {u: tinv[u] - jnp.dot(tb[u], y[u].astype(BF16), preferred_element_type=F32) for u in units}
    egc_all = [jnp.exp(gc_cols[sb]) for sb in range(nsub)]
    last_all = [gc_cols[sb][blk - 1:blk, :] for sb in range(nsub)]
    kdec_all = [jnp.exp(last_all[sb] - gc_cols[sb]) for sb in range(nsub)]
    slast_all = [jnp.exp(last_all[sb]) for sb in range(nsub)]
    egc = {(sb, h): egc_all[sb][:, h:h + 1] for sb, h in units}
    uw = {u: jnp.dot(tinv[u].astype(BF16),
                     jnp.concatenate([v[u] * beta[u], k[u] * (beta[u] * egc[u])], axis=1).astype(BF16),
                     preferred_element_type=F32) for u in units}
    o = {}
    for sb in range(nsub):
        sbf = [x.astype(BF16) for x in s_cur]
        ws = [jnp.dot(jnp.concatenate([uw[sb, h][:, GDN_DV:], q[sb, h] * egc[sb, h]], axis=0).astype(BF16),
                      sbf[h], preferred_element_type=F32) for h in heads]
        vb = [(uw[sb, h][:, 0:GDN_DV] - ws[h][0:blk]).astype(BF16) for h in heads]
        for h in heads:
            o[sb, h] = ws[h][blk:2 * blk] + jnp.dot((kq[sb, h][blk:2 * blk] * decay[sb, h]).astype(BF16), vb[h],
                                                    preferred_element_type=F32)
        s_cur = [s_cur[h] * slast_all[sb][:, h:h + 1]
                 + lax.dot_general((k[sb, h] * kdec_all[sb][:, h:h + 1]).astype(BF16), vb[h], TN_DIMS,
                                   preferred_element_type=F32) for h in heads]
    for h in heads:
        s_ref[h] = s_cur[h]
    for sb, h in units:
        o_ref[0, sb * blk:sb * blk + n_out, h * GDN_DV:(h + 1) * GDN_DV] = (
            _rms(o[sb, h][0:n_out], gout) * (zs[sb, h] * _sigmoid(zs[sb, h]))).astype(BF16)

    @pl.when(t == nt - 1)
    def _():
        sfin_ref[0] = s_ref[...]


def _gdn(proj3, tail0, s0, wc, a_log, dt_bias, g_out):
    nb, t, _ = proj3.shape
    blk = GDN_BLOCK
    valid = min(t, blk)
    nsub = GDN_BLOCKS_PER_STEP if t % (GDN_BLOCKS_PER_STEP * blk) == 0 else 1
    rin = nsub * valid
    rows = nsub * blk
    ab = proj3[:, :, COL_AB:COL_AB + 2 * GDN_HEADS].reshape(nb, t // rin, rin, 2 * GDN_HEADS)
    ab_rows = jnp.pad(ab.transpose(0, 1, 3, 2), ((0, 0), (0, 0), (0, 0), (0, rows - rin)))
    pad_c = lambda v: jnp.zeros((1, LANES), F32).at[0, :GDN_HEADS].set(v)
    pad_r = lambda v: jnp.zeros((2 * GDN_HEADS, 1), F32).at[:GDN_HEADS, 0].set(v)
    const2 = lambda b, i: (0, 0)
    return pl.pallas_call(
        functools.partial(_gdn_kernel, blk=blk, nsub=nsub, valid=valid, n_levels=int(math.log2(blk))),
        grid=(nb, t // rin),
        in_specs=[pl.BlockSpec((1, rin, GDN_CONV_DIM), lambda b, i: (b, i, COL_QKV // GDN_CONV_DIM)),
                  pl.BlockSpec((1, rin, GDN_OUT), lambda b, i: (b, i, COL_Z // GDN_OUT)),
                  pl.BlockSpec((1, rin, LANES), lambda b, i: (b, i, COL_AB // LANES)),
                  pl.BlockSpec((1, 1, 2 * GDN_HEADS, rows), lambda b, i: (b, i, 0, 0)),
                  pl.BlockSpec((1, SUBLANES, GDN_CONV_DIM), lambda b, i: (b, 0, 0)),
                  pl.BlockSpec((1, GDN_HEADS, GDN_DK, GDN_DV), lambda b, i: (b, 0, 0, 0)),
                  pl.BlockSpec((GDN_CONV, GDN_CONV_DIM), const2),
                  pl.BlockSpec((1, LANES), const2),
                  pl.BlockSpec((1, LANES), const2),
                  pl.BlockSpec((2 * GDN_HEADS, 1), const2),
                  pl.BlockSpec((2 * GDN_HEADS, 1), const2),
                  pl.BlockSpec((1, GDN_DV), const2)],
        out_specs=[pl.BlockSpec((1, rin, GDN_OUT), lambda b, i: (b, i, 0)),
                   pl.BlockSpec((1, GDN_HEADS, GDN_DK, GDN_DV), lambda b, i: (b, 0, 0, 0))],
        out_shape=[jax.ShapeDtypeStruct((nb, t, GDN_OUT), BF16),
                   jax.ShapeDtypeStruct((nb, GDN_HEADS, GDN_DK, GDN_DV), F32)],
        scratch_shapes=[pltpu.VMEM((rows + 2 * SUBLANES, GDN_CONV_DIM), F32),
                        pltpu.VMEM((GDN_HEADS, GDN_DK, GDN_DV), F32)],
        compiler_params=_params(("arbitrary", "arbitrary"), VMEM_MIB['gdn']),
        name="gdn",
    )(proj3, proj3, proj3, ab_rows, tail0, s0, wc, pad_c(a_log), pad_c(dt_bias), pad_r(a_log), pad_r(dt_bias),
      g_out)


def _out_proj_kernel(oa_ref, ob_ref, w_ref, x_ref, y_ref):
    y_ref[...] = (x_ref[...]
                  + jnp.dot(oa_ref[...], w_ref[0:MLA_OUT, :], preferred_element_type=F32)
                  + jnp.dot(ob_ref[...], w_ref[MLA_OUT:MLA_OUT + GDN_OUT, :], preferred_element_type=F32))


def _out_proj(o_a, o_b, w, x, tm):
    m = x.shape[0]
    row = lambda i: (i, 0)
    return pl.pallas_call(
        _out_proj_kernel,
        grid=(m // tm,),
        in_specs=[pl.BlockSpec((tm, MLA_OUT), row),
                  pl.BlockSpec((tm, GDN_OUT), row),
                  pl.BlockSpec(w.shape, lambda i: (0, 0)),
                  pl.BlockSpec((tm, D_MODEL), row)],
        out_specs=pl.BlockSpec((tm, D_MODEL), row),
        out_shape=jax.ShapeDtypeStruct((m, D_MODEL), F32),
        compiler_params=_params(("arbitrary",), VMEM_MIB['out_proj']),
        name="out_proj",
    )(o_a, o_b, w, x)


def _ffn_kernel(x_ref, g_ref, halo0_ref, wg_ref, wu_ref, wc_ref, bc_ref, wd_ref, y_ref, st_ref, h_ref, ext_ref,
                carry_ref, *, tm, rows, halo, shift, per_seq):
    i = pl.program_id(0)
    f = pl.program_id(1)

    @pl.when(f == 0)
    def _():
        x = x_ref[...]
        h_ref[...] = _rms(x, g_ref[...]).astype(BF16)
        y_ref[...] = x

    @pl.when(i % per_seq == 0)
    def _():
        ext_ref[0:halo, :] = halo0_ref[0]

    @pl.when(i % per_seq != 0)
    def _():
        ext_ref[0:halo, :] = carry_ref[f]

    wc = wc_ref[...]
    bc = bc_ref[...]
    for r0 in range(0, tm, rows):
        h = h_ref[r0:r0 + rows, :]
        gate = jnp.dot(h, wg_ref[...], preferred_element_type=F32)
        up = jnp.dot(h, wu_ref[...], preferred_element_type=F32)
        ext_ref[halo + r0:halo + r0 + rows, :] = gate
        gc = (wc[2:3, :] * gate + wc[1:2, :] * ext_ref[halo + r0 - shift:halo + r0 - shift + rows, :]
              + wc[0:1, :] * ext_ref[halo + r0 - 2 * shift:halo + r0 - 2 * shift + rows, :] + bc)
        act = (gc * _sigmoid(gc)) * up
        y_ref[r0:r0 + rows, :] += jnp.dot(act.astype(BF16), wd_ref[...], preferred_element_type=F32)
    last = ext_ref[tm:tm + halo, :]
    carry_ref[f] = last
    st_ref[0] = last


def _ffn(x, g, halo0, wg, wu, wc, bc, wd, tm, tf, halo, shift, per_seq):
    m = x.shape[0]
    nf = D_FF // tf
    return pl.pallas_call(
        functools.partial(_ffn_kernel, tm=tm, rows=min(tm, FFN_ROW_CHUNK), halo=halo, shift=shift, per_seq=per_seq),
        grid=(m // tm, nf),
        in_specs=[pl.BlockSpec((tm, D_MODEL), lambda i, f: (i, 0), pipeline_mode=pl.Buffered(1)),
                  pl.BlockSpec((1, D_MODEL), lambda i, f: (0, 0)),
                  pl.BlockSpec((1, halo, tf), lambda i, f: (i // per_seq, 0, f)),
                  pl.BlockSpec((D_MODEL, tf), lambda i, f: (0, f)),
                  pl.BlockSpec((D_MODEL, tf), lambda i, f: (0, f)),
                  pl.BlockSpec((FFN_CONV, tf), lambda i, f: (0, f)),
                  pl.BlockSpec((1, tf), lambda i, f: (0, f)),
                  pl.BlockSpec((tf, D_MODEL), lambda i, f: (f, 0))],
        out_specs=[pl.BlockSpec((tm, D_MODEL), lambda i, f: (i, 0)),
                   pl.BlockSpec((1, halo, tf), lambda i, f: (i, 0, f))],
        out_shape=[jax.ShapeDtypeStruct((m, D_MODEL), F32),
                   jax.ShapeDtypeStruct((m // tm, halo, D_FF), F32)],
        scratch_shapes=[pltpu.VMEM((tm, D_MODEL), BF16),
                        pltpu.VMEM((halo + tm, tf), F32),
                        pltpu.VMEM((nf, halo, tf), F32)],
        compiler_params=_params(("arbitrary", "arbitrary"), VMEM_MIB['ffn']),
        name="conv_ffn",
    )(x, g, halo0, wg, wu, wc, bc, wd)


def _rope_tables(pos, reps):
    half = QK_ROPE // 2
    inv = 1.0 / (ROPE_THETA ** (jnp.arange(half, dtype=F32) / half))
    ang = pos.astype(F32)[:, None] * inv[None, :]
    cos, sin = jnp.cos(ang), jnp.sin(ang)
    cos = jnp.tile(jnp.concatenate([cos, cos], axis=-1), (reps, LANES // QK_ROPE))
    sin = jnp.tile(jnp.concatenate([-sin, sin], axis=-1), (reps, LANES // QK_ROPE))
    return cos, sin


def _prep_weights(lw):
    w_in = lw['w_in']
    off = np.cumsum([Q_LORA, KV_LORA, QK_ROPE, GDN_CONV_DIM, GDN_OUT, GDN_HEADS, GDN_HEADS]).tolist()
    wt = w_in.T
    zr = lambda n: jnp.zeros((n, D_MODEL), w_in.dtype)
    w_in_r = jnp.concatenate([wt[off[2]:off[4]], wt[:off[1]], wt[off[1]:off[2]], zr(LANES - QK_ROPE),
                              wt[off[4]:off[6]], zr(LANES - 2 * GDN_HEADS)], axis=0)
    wq = lw['w_q_up'].reshape(Q_LORA, MLA_HEADS, QK_DIM)
    wq_r = jnp.concatenate([wq[:, :, :QK_NOPE].reshape(Q_LORA, -1), wq[:, :, QK_NOPE:].reshape(Q_LORA, -1)], axis=1)
    wkv = lw['w_kv_up']
    row = lambda v: v.reshape(1, -1).astype(F32)
    return dict(
        w_in=w_in_r.astype(BF16), wq=wq_r.astype(BF16),
        wk=wkv[:, :, :QK_NOPE].reshape(KV_LORA, -1).astype(BF16),
        wv=wkv[:, :, QK_NOPE:].reshape(KV_LORA, -1).astype(BF16),
        w_out=lw['w_out'].astype(BF16), wg=lw['w_ffn_gate'].astype(BF16), wu=lw['w_ffn_up'].astype(BF16),
        wd=lw['w_ffn_down'].astype(BF16),
        g_attn=row(lw['g_attn_norm']), g_q_lat=row(lw['g_q_lat']), g_kv_lat=row(lw['g_kv_lat']),
        g_q_nope=row(lw['g_q_nope']), g_k_nope=row(lw['g_k_nope']),
        g_q_rope2=row(jnp.tile(lw['g_q_rope'], LANES // QK_ROPE)),
        g_k_rope2=row(jnp.tile(lw['g_k_rope'], LANES // QK_ROPE)),
        wc_gdn=lw['w_gdn_conv'].astype(F32), a_log=lw['a_log'].astype(F32), dt_bias=lw['dt_bias'].astype(F32),
        g_gdn_out=row(lw['g_gdn_out']), g_ffn=row(lw['g_ffn_norm']), wc_ffn=lw['w_ffn_conv'].astype(F32),
        bc_ffn=row(lw['b_ffn_conv']))


def _pad_rows_front(a, rows):
    return jnp.pad(a, ((0, 0), (rows - a.shape[1], 0), (0, 0)))


def _prompt_layer(x, w):
    nb, lb, _ = x.shape
    m = nb * lb
    xf = x.reshape(m, D_MODEL)
    proj = _norm_matmul(xf, w['g_attn'], w['w_in'], IN_PROJ_ROWS, IN_PROJ_COLS)
    cos, sin = _rope_tables(jnp.arange(lb), 1)
    c_kv, k_rope, qf, kf, v = _mla_pre(proj, cos, sin, w['g_q_lat'], w['g_kv_lat'], w['wq'], w['g_q_nope'],
                                       w['g_q_rope2'], w['g_k_rope2'], nb, lb, MLA_ROWS,
                                       kv_weights=(w['wk'], w['wv'], w['g_k_nope']))
    o_a = _flash_attention(qf, kf, v, FLASH_Q_ROWS, FLASH_KEY_ROWS, FLASH_ROW_GROUP).reshape(m, MLA_OUT)

    proj3 = proj.reshape(nb, lb, D_IN_PAD)
    tail0 = jnp.zeros((nb, SUBLANES, GDN_CONV_DIM), F32)
    s0 = jnp.zeros((nb, GDN_HEADS, GDN_DK, GDN_DV), F32)
    o_b, s_new = _gdn(proj3, tail0, s0, w['wc_gdn'], w['a_log'], w['dt_bias'], w['g_gdn_out'])
    x1 = _out_proj(o_a, o_b.reshape(m, GDN_OUT), w['w_out'], xf, OUT_PROJ_ROWS)

    tm = FFN_ROWS
    halo0 = jnp.zeros((nb, SUBLANES, D_FF), F32)
    y, gate_tail = _ffn(x1, w['g_ffn'], halo0, w['wg'], w['wu'], w['wc_ffn'], w['bc_ffn'], w['wd'],
                        tm, FFN_COLS, SUBLANES, 1, lb // tm)
    state = (c_kv.reshape(nb, lb, KV_LORA), k_rope.reshape(nb, lb, QK_ROPE),
             proj3[:, lb - (GDN_CONV - 1):, COL_QKV:COL_QKV + GDN_CONV_DIM], s_new,
             gate_tail.reshape(nb, lb // tm, SUBLANES, D_FF)[:, -1, SUBLANES - (FFN_CONV - 1):, :])
    return y.reshape(nb, lb, D_MODEL), state


def _sample_layer(x, lat_past, krope_past, conv_past, s_past, ffn_past, w):
    nb, lb, _ = x.shape
    past = lat_past.shape[1]
    assert (past + lb - 1) // CHUNK == past // CHUNK and past % CHUNK == 0, "new frames must share one chunk"
    m = nb * lb
    xf = x.reshape(m, D_MODEL)
    proj = _norm_matmul(xf, w['g_attn'], w['w_in'], m, IN_PROJ_COLS_SMALL)
    cos, sin = _rope_tables(past + jnp.arange(lb), nb)
    c_kv, k_rope, qf = _mla_pre(proj, cos, sin, w['g_q_lat'], w['g_kv_lat'], w['wq'], w['g_q_nope'],
                                w['g_q_rope2'], w['g_k_rope2'], 1, m, m)
    qh = qf[0].reshape(MLA_HEADS, nb, lb, QK_DIM).transpose(1, 0, 2, 3)
    qn_bd = jnp.einsum('bhqd,hg->bhqgd', qh[..., :QK_NOPE], jnp.eye(MLA_HEADS, dtype=qh.dtype))
    qn_bd = qn_bd.reshape(nb, MLA_HEADS * lb, MLA_HEADS * QK_NOPE)
    qr = qh[..., QK_NOPE:].reshape(nb, MLA_HEADS * lb, QK_ROPE)
    o_a = _attn_cached(qn_bd, qr, lat_past, krope_past, c_kv, k_rope, w['wk'], w['wv'], w['g_k_nope'], lb,
                       DECODE_KEY_ROWS)

    proj3 = proj.reshape(nb, lb, D_IN_PAD)
    tail0 = _pad_rows_front(conv_past.astype(F32), SUBLANES)
    o_b, s_new = _gdn(proj3, tail0, s_past.astype(F32), w['wc_gdn'], w['a_log'], w['dt_bias'], w['g_gdn_out'])
    x1 = _out_proj(o_a, o_b.reshape(m, GDN_OUT), w['w_out'], xf, m)

    x1t = x1.reshape(nb, lb, D_MODEL).transpose(1, 0, 2).reshape(m, D_MODEL)
    n_hist = FFN_CONV - 1
    halo0 = ffn_past.astype(F32).transpose(1, 0, 2).reshape(1, n_hist * nb, D_FF)
    yt, gate_tail = _ffn(x1t, w['g_ffn'], halo0, w['wg'], w['wu'], w['wc_ffn'], w['bc_ffn'], w['wd'],
                         m, FFN_COLS, n_hist * nb, nb, 1)
    y = yt.reshape(lb, nb, D_MODEL).transpose(1, 0, 2)
    state = (c_kv.reshape(nb, lb, KV_LORA), k_rope.reshape(nb, lb, QK_ROPE),
             proj3[:, lb - (GDN_CONV - 1):, COL_QKV:COL_QKV + GDN_CONV_DIM], s_new,
             gate_tail.reshape(n_hist, nb, D_FF).transpose(1, 0, 2))
    return y, state


def kernel(x_prompt, x_sample, cache_mla_latent, cache_mla_krope, state_gdn_conv, state_gdn_S, state_ffn_conv,
           g_attn_norm, w_in, g_q_lat, g_kv_lat, w_q_up, w_kv_up, g_q_nope, g_q_rope, g_k_nope, g_k_rope,
           w_gdn_conv, a_log, dt_bias, g_gdn_out, w_out, g_ffn_norm, w_ffn_gate, w_ffn_up, w_ffn_conv,
           b_ffn_conv, w_ffn_down):
    xp, xs = x_prompt, x_sample
    new_p, new_s = [], []
    for l in range(w_in.shape[0]):
        w = _prep_weights(dict(
            g_attn_norm=g_attn_norm[l], w_in=w_in[l], g_q_lat=g_q_lat[l], g_kv_lat=g_kv_lat[l], w_q_up=w_q_up[l],
            w_kv_up=w_kv_up[l], g_q_nope=g_q_nope[l], g_q_rope=g_q_rope[l], g_k_nope=g_k_nope[l],
            g_k_rope=g_k_rope[l], w_gdn_conv=w_gdn_conv[l], a_log=a_log[l], dt_bias=dt_bias[l],
            g_gdn_out=g_gdn_out[l], w_out=w_out[l], g_ffn_norm=g_ffn_norm[l], w_ffn_gate=w_ffn_gate[l],
            w_ffn_up=w_ffn_up[l], w_ffn_conv=w_ffn_conv[l], b_ffn_conv=b_ffn_conv[l], w_ffn_down=w_ffn_down[l]))
        xp, st_p = _prompt_layer(xp, w)
        xs, st_s = _sample_layer(xs, cache_mla_latent[l], cache_mla_krope[l], state_gdn_conv[l], state_gdn_S[l],
                                 state_ffn_conv[l], w)
        new_p.append(st_p)
        new_s.append(st_s)
    p_state = [jnp.stack(t) for t in zip(*new_p)]
    s_state = [jnp.stack(t) for t in zip(*new_s)]
    return (xp, xs, *p_state, *s_state)
```
